```python
import jax, jax.numpy as jnp
from jax import lax
import numpy as np

D_MODEL = 2048
BATCH = 4
SEQ = 4096
DEPTH = 1

GRID_W = 64
CTX_LEN = 256
HEAD_DIM = 128
N_Q_HEADS = 16
N_KV_HEADS = 4
GQA_GROUP = N_Q_HEADS // N_KV_HEADS
ATTN_W = N_Q_HEADS * HEAD_DIM
KV_W = N_KV_HEADS * HEAD_DIM
LRU_W = D_MODEL
LRU_BLOCKS = 16
LRU_BLOCK_DIM = LRU_W // LRU_BLOCKS
LRU_C = 8.0
CONV_W = 4
CONV_LEFT = 2
D_FF = 5632
Q_BLOCK = 128
ROPE_THETA = 10000.0
EPS = 1e-6
N_MOD = 9
FFN_RES = 0.5
OFF_Q = 0
OFF_K = OFF_Q + ATTN_W
OFF_V = OFF_K + KV_W
OFF_LX = OFF_V + KV_W
OFF_LG = OFF_LX + LRU_W
OFF_GA = OFF_LG + LRU_W
OFF_GL = OFF_GA + D_MODEL
IN_W = OFF_GL + D_MODEL

kernel_name = 'hybrid_gqa_rglru_macaron_dit_layer'


def rms_norm(t, g):
    tf = t.astype(jnp.float32)
    y = tf * lax.rsqrt(jnp.mean(tf * tf, axis=-1, keepdims=True) + EPS)
    return (y * g.astype(jnp.float32)).astype(t.dtype)


def modulate(h, shift, scale):
    return h * (1.0 + scale) + shift


def swiglu(h, wg, wu, wd):
    return (jax.nn.silu(h @ wg) * (h @ wu)) @ wd


def axial_rope_tables(n_tok):
    rows = n_tok // GRID_W
    row = jnp.repeat(jnp.arange(rows, dtype=jnp.float32), GRID_W)
    col = jnp.tile(jnp.arange(GRID_W, dtype=jnp.float32), rows)
    axis_dims = HEAD_DIM // 2
    freqs = ROPE_THETA ** (-jnp.arange(0, axis_dims, 2, dtype=jnp.float32) / axis_dims)
    ang = jnp.concatenate([row[:, None] * freqs, col[:, None] * freqs], axis=-1)
    return jnp.cos(ang), jnp.sin(ang)


def apply_rope(t, cos, sin):
    tf = t.astype(jnp.float32).reshape(t.shape[:-1] + (HEAD_DIM // 2, 2))
    t1, t2 = tf[..., 0], tf[..., 1]
    out = jnp.stack([t1 * cos - t2 * sin, t1 * sin + t2 * cos], axis=-1)
    return out.reshape(t.shape).astype(t.dtype)


def to_heads(t, n_heads):
    b, n, _ = t.shape
    return t.reshape(b, n, n_heads, HEAD_DIM).transpose(0, 2, 1, 3)


def group_queries(q):
    b, _, n, _ = q.shape
    return q.reshape(b, N_KV_HEADS, GQA_GROUP, n, HEAD_DIM)


def latent_attention(q, k_lat, v_lat, k_ctx, v_ctx):
    b, _, _, n, _ = q.shape
    k_all = jnp.concatenate([k_ctx, k_lat], axis=2)
    v_all = jnp.concatenate([v_ctx, v_lat], axis=2)
    n_blk = n // Q_BLOCK
    qb = jnp.moveaxis(q.reshape(b, N_KV_HEADS, GQA_GROUP, n_blk, Q_BLOCK, HEAD_DIM), 3, 0)
    scale = HEAD_DIM ** -0.5

    def one_block(q_blk):
        s = jnp.einsum('bkgqd,bksd->bkgqs', q_blk, k_all, preferred_element_type=jnp.float32) * scale
        p = jax.nn.softmax(s, axis=-1)
        return jnp.einsum('bkgqs,bksd->bkgqd', p.astype(v_all.dtype), v_all)

    ob = lax.map(one_block, qb)
    return ob.transpose(1, 0, 4, 2, 3, 5).reshape(b, n, ATTN_W)


def context_attention(q, k, v):
    b, _, _, n, _ = q.shape
    s = jnp.einsum('bkgqd,bksd->bkgqs', q, k, preferred_element_type=jnp.float32) * (HEAD_DIM ** -0.5)
    p = jax.nn.softmax(s, axis=-1)
    o = jnp.einsum('bkgqs,bksd->bkgqd', p.astype(v.dtype), v)
    return o.transpose(0, 3, 1, 2, 4).reshape(b, n, ATTN_W)


def centred_dwconv(t, w, b):
    n = t.shape[1]
    tp = jnp.pad(t, ((0, 0), (CONV_LEFT, CONV_W - 1 - CONV_LEFT), (0, 0)))
    out = b
    for k in range(CONV_W):
        out = out + tp[:, k:k + n] * w[k]
    return out


def block_diag(t, w, b):
    tb = t.reshape(t.shape[:-1] + (LRU_BLOCKS, LRU_BLOCK_DIM))
    return jnp.einsum('btnd,nde->btne', tb, w).reshape(t.shape) + b


def rglru_coeffs(xc, w_a, b_a, w_x, b_x, lam):
    xf = xc.astype(jnp.float32)
    r = jax.nn.sigmoid(block_diag(xf, w_a, b_a).astype(jnp.float32))
    i = jax.nn.sigmoid(block_diag(xf, w_x, b_x).astype(jnp.float32))
    log_a = -LRU_C * r * jax.nn.softplus(-lam.astype(jnp.float32))
    a = jnp.exp(log_a)
    u = jnp.sqrt(-jnp.expm1(2.0 * log_a)) * (i * xf)
    return a, u


def linear_scan(a, u, h0, reverse):
    def combine(e1, e2):
        a1, b1 = e1
        a2, b2 = e2
        return a1 * a2, a2 * b1 + b2
    a_cum, b_cum = lax.associative_scan(combine, (a, u), axis=1, reverse=reverse)
    return a_cum * h0[:, None, :] + b_cum


def rglru_bidir(xc, h0_f, h0_b, wa, ba, wx, bx, lam):
    a_f, u_f = rglru_coeffs(xc, wa[0], ba[0], wx[0], bx[0], lam[0])
    a_b, u_b = rglru_coeffs(xc, wa[1], ba[1], wx[1], bx[1], lam[1])
    return linear_scan(a_f, u_f, h0_f, False), linear_scan(a_b, u_b, h0_b, True)


def gated_lru_out(h_f, h_b, gate, dtype):
    return ((h_f + h_b) * jax.nn.gelu(gate.astype(jnp.float32))).astype(dtype)


def merge_branches(attn, lru, ga, gl, w_out):
    return (jax.nn.sigmoid(ga) * attn + jax.nn.sigmoid(gl) * lru) @ w_out


def hybrid_layer(x, ctx, c, c_ctx, w_mod, b_mod, norm_g, ffn_wg, ffn_wu, ffn_wd, w_in, w_out,
                 q_norm_g, k_norm_g, conv_w, conv_b, lru_wa, lru_ba, lru_wx, lru_bx, lru_lambda,
                 cos, sin, update_ctx):
    b = x.shape[0]
    mod_x = (jax.nn.silu(c) @ w_mod + b_mod).reshape(b, N_MOD, 1, D_MODEL)
    mod_c = (jax.nn.silu(c_ctx) @ w_mod + b_mod).reshape(N_MOD, D_MODEL)
    sh1, sc1, g1, sh2, sc2, g2, sh3, sc3, g3 = [mod_x[:, i] for i in range(N_MOD)]
    csh1, csc1, cg1, csh2, csc2, cg2, csh3, csc3, cg3 = [mod_c[i] for i in range(N_MOD)]

    x = x + FFN_RES * g1 * swiglu(modulate(rms_norm(x, norm_g[0]), sh1, sc1), ffn_wg[0], ffn_wu[0], ffn_wd[0])
    ctx = ctx + FFN_RES * cg1 * swiglu(modulate(rms_norm(ctx, norm_g[0]), csh1, csc1), ffn_wg[0], ffn_wu[0], ffn_wd[0])

    hx = modulate(rms_norm(x, norm_g[1]), sh2, sc2)
    hc = modulate(rms_norm(ctx, norm_g[1]), csh2, csc2)

    pc = hc @ w_in[:, OFF_K:OFF_LG]
    k_c = rms_norm(to_heads(pc[..., :KV_W], N_KV_HEADS), k_norm_g)
    v_c = to_heads(pc[..., KV_W:2 * KV_W], N_KV_HEADS)
    xc_c = centred_dwconv(pc[..., 2 * KV_W:], conv_w, conv_b)
    zeros = jnp.zeros((b, LRU_W), jnp.float32)
    hf_c, hb_c = rglru_bidir(xc_c, zeros, zeros, lru_wa, lru_ba, lru_wx, lru_bx, lru_lambda)

    p = hx @ w_in
    q = apply_rope(rms_norm(to_heads(p[..., OFF_Q:OFF_K], N_Q_HEADS), q_norm_g), cos, sin)
    k = apply_rope(rms_norm(to_heads(p[..., OFF_K:OFF_V], N_KV_HEADS), k_norm_g), cos, sin)
    v = to_heads(p[..., OFF_V:OFF_LX], N_KV_HEADS)
    attn = latent_attention(group_queries(q), k, v, k_c, v_c)
    xc = centred_dwconv(p[..., OFF_LX:OFF_LG], conv_w, conv_b)
    hf, hb = rglru_bidir(xc, hf_c[:, -1], hb_c[:, 0], lru_wa, lru_ba, lru_wx, lru_bx, lru_lambda)
    lru = gated_lru_out(hf, hb, p[..., OFF_LG:OFF_GA], x.dtype)
    x = x + g2 * merge_branches(attn, lru, p[..., OFF_GA:OFF_GL], p[..., OFF_GL:], w_out)

    if update_ctx:
        pq_c = hc @ w_in[:, OFF_Q:OFF_K]
        pg_c = hc @ w_in[:, OFF_LG:]
        q_c = rms_norm(to_heads(pq_c, N_Q_HEADS), q_norm_g)
        attn_c = context_attention(group_queries(q_c), k_c, v_c)
        lru_c = gated_lru_out(hf_c, hb_c, pg_c[..., :LRU_W], ctx.dtype)
        ctx = ctx + cg2 * merge_branches(attn_c, lru_c, pg_c[..., LRU_W:LRU_W + D_MODEL],
                                         pg_c[..., LRU_W + D_MODEL:], w_out)

    x = x + FFN_RES * g3 * swiglu(modulate(rms_norm(x, norm_g[2]), sh3, sc3), ffn_wg[1], ffn_wu[1], ffn_wd[1])
    if update_ctx:
        ctx = ctx + FFN_RES * cg3 * swiglu(modulate(rms_norm(ctx, norm_g[2]), csh3, csc3), ffn_wg[1], ffn_wu[1], ffn_wd[1])
    return x, ctx


def setup_inputs(seed: int = 0) -> dict:
    key = jax.random.key(seed)
    ks = jax.random.split(key, 24)
    f32 = jnp.float32

    def nrm(k, shape, scale):
        return jax.random.normal(k, shape, f32) * scale

    u = jax.random.uniform(ks[20], (DEPTH, 2, LRU_W), f32, 0.9, 0.999)
    base = u ** (1.0 / LRU_C)
    return {
        'x': nrm(ks[0], (BATCH, SEQ, D_MODEL), 1.0),
        'c': nrm(ks[1], (BATCH, D_MODEL), 1.0),
        'ctx': nrm(ks[2], (BATCH, CTX_LEN, D_MODEL), 1.0),
        'c_ctx': nrm(ks[3], (D_MODEL,), 1.0),
        'w_mod': nrm(ks[4], (DEPTH, D_MODEL, N_MOD * D_MODEL), 0.5 * D_MODEL ** -0.5),
        'b_mod': nrm(ks[5], (DEPTH, N_MOD * D_MODEL), 0.01),
        'norm_g': 1.0 + nrm(ks[6], (DEPTH, 3, D_MODEL), 0.02),
        'ffn_wg': nrm(ks[7], (DEPTH, 2, D_MODEL, D_FF), D_MODEL ** -0.5),
        'ffn_wu': nrm(ks[8], (DEPTH, 2, D_MODEL, D_FF), D_MODEL ** -0.5),
        'ffn_wd': nrm(ks[9], (DEPTH, 2, D_FF, D_MODEL), D_FF ** -0.5),
        'w_in': nrm(ks[10], (DEPTH, D_MODEL, IN_W), D_MODEL ** -0.5),
        'w_out': nrm(ks[11], (DEPTH, D_MODEL, D_MODEL), D_MODEL ** -0.5),
        'q_norm_g': 1.0 + nrm(ks[12], (DEPTH, HEAD_DIM), 0.02),
        'k_norm_g': 1.0 + nrm(ks[13], (DEPTH, HEAD_DIM), 0.02),
        'conv_w': nrm(ks[14], (DEPTH, CONV_W, LRU_W), CONV_W ** -0.5),
        'conv_b': nrm(ks[15], (DEPTH, LRU_W), 0.01),
        'lru_wa': nrm(ks[16], (DEPTH, 2, LRU_BLOCKS, LRU_BLOCK_DIM, LRU_BLOCK_DIM), LRU_BLOCK_DIM ** -0.5),
        'lru_ba': nrm(ks[17], (DEPTH, 2, LRU_W), 0.01),
        'lru_wx': nrm(ks[18], (DEPTH, 2, LRU_BLOCKS, LRU_BLOCK_DIM, LRU_BLOCK_DIM), LRU_BLOCK_DIM ** -0.5),
        'lru_bx': nrm(ks[19], (DEPTH, 2, LRU_W), 0.01),
        'lru_lambda': jnp.log(base) - jnp.log1p(-base),
        'final_norm_g': 1.0 + nrm(ks[21], (D_MODEL,), 0.02),
    }


def reference(x, c, ctx, c_ctx, w_mod, b_mod, norm_g, ffn_wg, ffn_wu, ffn_wd, w_in, w_out,
              q_norm_g, k_norm_g, conv_w, conv_b, lru_wa, lru_ba, lru_wx, lru_bx, lru_lambda,
              final_norm_g):
    cos, sin = axial_rope_tables(x.shape[1])
    for l in range(DEPTH):
        x, ctx = hybrid_layer(x, ctx, c, c_ctx, w_mod[l], b_mod[l], norm_g[l], ffn_wg[l], ffn_wu[l], ffn_wd[l],
                              w_in[l], w_out[l], q_norm_g[l], k_norm_g[l], conv_w[l], conv_b[l],
                              lru_wa[l], lru_ba[l], lru_wx[l], lru_bx[l], lru_lambda[l],
                              cos, sin, l < DEPTH - 1)
    return rms_norm(x, final_norm_g)
```

```python
import functools

import jax
import jax.numpy as jnp
from jax import lax
from jax.experimental import pallas as pl
from jax.experimental.pallas import tpu as pltpu

D_MODEL = 2048
GRID_W = 64
HEAD_DIM = 128
N_Q_HEADS = 16
N_KV_HEADS = 4
GQA_GROUP = N_Q_HEADS // N_KV_HEADS
ATTN_W = N_Q_HEADS * HEAD_DIM
KV_W = N_KV_HEADS * HEAD_DIM
LRU_W = D_MODEL
LRU_BLOCK_DIM = 128
LRU_C = 8.0
D_FF = 5632
ROPE_THETA = 10000.0
EPS = 1e-6
N_MOD = 9
FFN_RES = 0.5
OFF_Q = 0
OFF_K = OFF_Q + ATTN_W
OFF_V = OFF_K + KV_W
OFF_LX = OFF_V + KV_W
OFF_LG = OFF_LX + LRU_W
OFF_GA = OFF_LG + LRU_W
OFF_GL = OFF_GA + D_MODEL
IN_W = OFF_GL + D_MODEL

SUBLANES = 8
LANES = 128
VMEM_LIMIT = 56 * 1024 * 1024

BF16 = jnp.bfloat16
F32 = jnp.float32


def _params(*sem):
    return pltpu.CompilerParams(dimension_semantics=sem, vmem_limit_bytes=VMEM_LIMIT)


def _dot(a, b):
    return jnp.dot(a, b, preferred_element_type=F32)


def _rms(x):
    return x * lax.rsqrt(jnp.mean(x * x, axis=-1, keepdims=True) + EPS)


MOD_TN = 1024


def _mod_kernel(c_ref, w_ref, b_ref, o_ref):
    c = c_ref[...]
    s = c * jax.nn.sigmoid(c)
    o_ref[...] = jnp.dot(s, w_ref[...], preferred_element_type=F32,
                         precision=lax.Precision.HIGHEST) + b_ref[...]


def _modulation(cc, w_mod, b_mod):
    rows, d = cc.shape
    n = w_mod.shape[1]
    return pl.pallas_call(
        _mod_kernel,
        grid=(n // MOD_TN,),
        in_specs=[
            pl.BlockSpec((rows, d), lambda j: (0, 0)),
            pl.BlockSpec((d, MOD_TN), lambda j: (0, j)),
            pl.BlockSpec((1, MOD_TN), lambda j: (0, j)),
        ],
        out_specs=pl.BlockSpec((rows, MOD_TN), lambda j: (0, j)),
        out_shape=jax.ShapeDtypeStruct((rows, n), F32),
        compiler_params=_params("arbitrary"),
        name="modulation",
    )(cc, w_mod, b_mod.reshape(1, n))


FFN_TF = 512


def _ffn_kernel(x_ref, mod_ref, g_ref, wg_ref, wu_ref, wd_ref, fg_ref, o_ref, h_scr, acc_scr,
                *, mod_base, final_norm):
    j = pl.program_id(2)

    @pl.when(j == 0)
    def _():
        y = _rms(x_ref[0]) * g_ref[...]
        shift = mod_ref[0, mod_base:mod_base + 1, :]
        scale = mod_ref[0, mod_base + 1:mod_base + 2, :]
        h_scr[...] = (y * (1.0 + scale) + shift).astype(BF16)
        acc_scr[...] = jnp.zeros_like(acc_scr)

    h = h_scr[...]
    gate = _dot(h, wg_ref[...])
    up = _dot(h, wu_ref[...])
    act = (gate * jax.nn.sigmoid(gate) * up).astype(BF16)
    acc_scr[...] += _dot(act, wd_ref[...])

    @pl.when(j == pl.num_programs(2) - 1)
    def _():
        g = mod_ref[0, mod_base + 2:mod_base + 3, :]
        xo = x_ref[0] + FFN_RES * g * acc_scr[...]
        if final_norm:
            xo = _rms(xo) * fg_ref[...]
        o_ref[0] = xo


def _ffn(x, mod, mod_row, norm_g, wg, wu, wd, final_g, *, ffn_idx, mod_base, final_norm, tm):
    b, s, d = x.shape
    kern = functools.partial(_ffn_kernel, mod_base=mod_base, final_norm=final_norm)
    return pl.pallas_call(
        kern,
        grid=(b, s // tm, D_FF // FFN_TF),
        in_specs=[
            pl.BlockSpec((1, tm, d), lambda bb, i, j: (bb, i, 0)),
            pl.BlockSpec((1, N_MOD, d), lambda bb, i, j: (mod_row(bb), 0, 0)),
            pl.BlockSpec((1, d), lambda bb, i, j: (0, 0)),
            pl.BlockSpec((None, d, FFN_TF), lambda bb, i, j: (ffn_idx, 0, j)),
            pl.BlockSpec((None, d, FFN_TF), lambda bb, i, j: (ffn_idx, 0, j)),
            pl.BlockSpec((None, FFN_TF, d), lambda bb, i, j: (ffn_idx, j, 0)),
            pl.BlockSpec((1, d), lambda bb, i, j: (0, 0)),
        ],
        out_specs=pl.BlockSpec((1, tm, d), lambda bb, i, j: (bb, i, 0)),
        out_shape=jax.ShapeDtypeStruct((b, s, d), F32),
        scratch_shapes=[pltpu.VMEM((tm, d), BF16), pltpu.VMEM((tm, d), F32)],
        compiler_params=_params("parallel", "parallel", "arbitrary"),
        name="ffn",
    )(x, mod, norm_g.reshape(1, d), wg, wu, wd, final_g.reshape(1, d))


PROJ_TN = 1024
PROJ_TILE_KV = OFF_K // PROJ_TN


def _proj_kernel(x_ref, mod_ref, g_ref, w_ref, qkg_ref, cos_ref, sin_ref, o_ref, h_scr,
                 *, mod_base, col0, rope):
    jj = pl.program_id(2)
    j = jj + col0

    @pl.when(jj == 0)
    def _():
        y = _rms(x_ref[0]) * g_ref[...]
        shift = mod_ref[0, mod_base:mod_base + 1, :]
        scale = mod_ref[0, mod_base + 1:mod_base + 2, :]
        h_scr[...] = (y * (1.0 + scale) + shift).astype(BF16)

    z = _dot(h_scr[...], w_ref[...])

    def head_norm(t, gain):
        y = _rms(t) * gain
        if rope:
            y = y * cos_ref[...] + pltpu.roll(y, HEAD_DIM // 2, axis=1) * sin_ref[...]
        return y.astype(BF16)

    def store_heads(n_heads, gain):
        for hh in range(n_heads):
            sl = slice(hh * HEAD_DIM, (hh + 1) * HEAD_DIM)
            o_ref[0, :, sl] = head_norm(z[:, sl], gain)

    @pl.when(j < PROJ_TILE_KV)
    def _():
        store_heads(PROJ_TN // HEAD_DIM, qkg_ref[0:1, :])

    @pl.when(j == PROJ_TILE_KV)
    def _():
        store_heads(N_KV_HEADS, qkg_ref[1:2, :])
        o_ref[0, :, KV_W:] = z[:, KV_W:].astype(BF16)

    @pl.when(j > PROJ_TILE_KV)
    def _():
        o_ref[0] = z.astype(BF16)


def _proj(x, mod, mod_row, norm_g, w_in, qkg, cosf, sinf, *, col0, ncol, rope, tm):
    b, s, d = x.shape
    kern = functools.partial(_proj_kernel, mod_base=3, col0=col0, rope=rope)
    return pl.pallas_call(
        kern,
        grid=(b, s // tm, ncol),
        in_specs=[
            pl.BlockSpec((1, tm, d), lambda bb, i, j: (bb, i, 0)),
            pl.BlockSpec((1, N_MOD, d), lambda bb, i, j: (mod_row(bb), 0, 0)),
            pl.BlockSpec((1, d), lambda bb, i, j: (0, 0)),
            pl.BlockSpec((d, PROJ_TN), lambda bb, i, j: (0, j + col0)),
            pl.BlockSpec((2, HEAD_DIM), lambda bb, i, j: (0, 0)),
            pl.BlockSpec((tm, HEAD_DIM), lambda bb, i, j: (i, 0)),
            pl.BlockSpec((tm, HEAD_DIM), lambda bb, i, j: (i, 0)),
        ],
        out_specs=pl.BlockSpec((1, tm, PROJ_TN), lambda bb, i, j: (bb, i, j)),
        out_shape=jax.ShapeDtypeStruct((b, s, ncol * PROJ_TN), BF16),
        scratch_shapes=[pltpu.VMEM((tm, d), BF16)],
        compiler_params=_params("parallel", "parallel", "arbitrary"),
        name="proj",
    )(x, mod, norm_g.reshape(1, d), w_in, qkg, cosf, sinf)


ATTN_TQ = 256


def _attn_kernel(q_ref, k_ref, v_ref, kc_ref, vc_ref, o_ref):
    k = k_ref[0]
    v = v_ref[0]
    kc = kc_ref[0]
    vc = vc_ref[0]
    nt = (((1,), (1,)), ((), ()))
    for hh in range(GQA_GROUP):
        sl = slice(hh * HEAD_DIM, (hh + 1) * HEAD_DIM)
        q = q_ref[0, :, sl]
        s1 = lax.dot_general(q, k, nt, preferred_element_type=F32)
        s2 = lax.dot_general(q, kc, nt, preferred_element_type=F32)
        m = jnp.maximum(jnp.max(s1, axis=-1, keepdims=True), jnp.max(s2, axis=-1, keepdims=True))
        p1 = jnp.exp(s1 - m)
        p2 = jnp.exp(s2 - m)
        l = jnp.sum(p1, axis=-1, keepdims=True) + jnp.sum(p2, axis=-1, keepdims=True)
        o = _dot(p1.astype(BF16), v) + _dot(p2.astype(BF16), vc)
        o_ref[0, :, sl] = (o * (1.0 / l)).astype(BF16)


def _attention(p, pc):
    b, s, _ = p.shape
    c = pc.shape[1]
    gw = GQA_GROUP * HEAD_DIM
    kb = OFF_K // HEAD_DIM
    vb = OFF_V // HEAD_DIM
    return pl.pallas_call(
        _attn_kernel,
        grid=(b, N_KV_HEADS, s // ATTN_TQ),
        in_specs=[
            pl.BlockSpec((1, ATTN_TQ, gw), lambda bb, g, i: (bb, i, g)),
            pl.BlockSpec((1, s, HEAD_DIM), lambda bb, g, i: (bb, 0, kb + g)),
            pl.BlockSpec((1, s, HEAD_DIM), lambda bb, g, i: (bb, 0, vb + g)),
            pl.BlockSpec((1, c, HEAD_DIM), lambda bb, g, i: (bb, 0, g)),
            pl.BlockSpec((1, c, HEAD_DIM), lambda bb, g, i: (bb, 0, N_KV_HEADS + g)),
        ],
        out_specs=pl.BlockSpec((1, ATTN_TQ, gw), lambda bb, g, i: (bb, i, g)),
        out_shape=jax.ShapeDtypeStruct((b, s, ATTN_W), BF16),
        compiler_params=_params("parallel", "parallel", "arbitrary"),
        name="attention",
    )(p, p, p, pc, pc)


LRU_CB = 512
LRU_TL = 256
LRU_PAD = LRU_TL + SUBLANES
HALO = SUBLANES
CONV_LEFT = 2


def _gelu_tanh(x):
    return 0.5 * x * (1.0 + jnp.tanh(0.7978845608028654 * (x + 0.044715 * x * x * x)))


def _lru_kernel(*refs, reverse, gated, nb):
    if gated:
        (x_ref, xp_ref, xn_ref, cw_ref, cb_ref, w_ref, ba_ref, bx_ref, lam_ref, h0_ref,
         hf_ref, lg_ref, o_ref, hl_ref, ext_scr, a_scr, u_scr, h_scr) = refs
    else:
        (x_ref, xp_ref, xn_ref, cw_ref, cb_ref, w_ref, ba_ref, bx_ref, lam_ref, h0_ref,
         o_ref, hl_ref, ext_scr, a_scr, u_scr, h_scr) = refs
    i = pl.program_id(1)
    n = pl.num_programs(1)
    ci = n - 1 - i if reverse else i
    tl = x_ref.shape[1]
    cb = x_ref.shape[2]

    @pl.when(i == 0)
    def _():
        for blk in range(cb // LRU_BLOCK_DIM):
            h_scr[blk] = h0_ref[:, blk * LRU_BLOCK_DIM:(blk + 1) * LRU_BLOCK_DIM]

    prev = xp_ref[...].astype(F32) * (ci > 0).astype(F32)
    nxt = xn_ref[...].astype(F32) * (ci < n - 1).astype(F32)
    ext_scr[:, 0:HALO, :] = prev
    ext_scr[:, HALO:HALO + tl, :] = x_ref[...].astype(F32)
    ext_scr[:, HALO + tl:, :] = nxt
    xc = cb_ref[...][None]
    for k in range(4):
        o = HALO - CONV_LEFT + k
        xc = xc + ext_scr[:, o:o + tl, :] * cw_ref[k:k + 1, :][None]
    xc = xc.reshape(nb * tl, cb)

    lam = lam_ref[...]
    neg = -lam
    sp = jnp.maximum(neg, 0.0) + jnp.log1p(jnp.exp(-jnp.abs(neg)))
    for blk in range(cb // LRU_BLOCK_DIM):
        sl = slice(blk * LRU_BLOCK_DIM, (blk + 1) * LRU_BLOCK_DIM)
        xb = xc[:, sl]
        z = _dot(xb.astype(BF16), w_ref[blk])
        r = jax.nn.sigmoid(z[:, :LRU_BLOCK_DIM] + ba_ref[:, sl])
        ig = jax.nn.sigmoid(z[:, LRU_BLOCK_DIM:] + bx_ref[:, sl])
        a = jnp.exp(-LRU_C * r * sp[:, sl])
        u = jnp.sqrt(1.0 - a * a) * (ig * xb)
        for bb in range(nb):
            a_scr[blk, bb * LRU_PAD:bb * LRU_PAD + tl, :] = a[bb * tl:(bb + 1) * tl]
            u_scr[blk, bb * LRU_PAD:bb * LRU_PAD + tl, :] = u[bb * tl:(bb + 1) * tl]

    nblk = cb // LRU_BLOCK_DIM

    def step(t, hs):
        tt = tl - 1 - t if reverse else t
        rows = pl.ds(tt, nb, stride=LRU_PAD)
        out = []
        for blk in range(nblk):
            h = a_scr[blk, rows, :] * hs[blk] + u_scr[blk, rows, :]
            u_scr[blk, rows, :] = h
            out.append(h)
        return tuple(out)

    hs = lax.fori_loop(0, tl, step, tuple(h_scr[blk] for blk in range(nblk)), unroll=8)
    for blk in range(nblk):
        h_scr[blk] = hs[blk]
        hl_ref[:, blk * LRU_BLOCK_DIM:(blk + 1) * LRU_BLOCK_DIM] = hs[blk]

    for bb in range(nb):
        hb = jnp.concatenate(
            [u_scr[blk, bb * LRU_PAD:bb * LRU_PAD + tl, :] for blk in range(nblk)], axis=-1)
        if gated:
            hb = (hb + hf_ref[bb].astype(F32)) * _gelu_tanh(lg_ref[bb].astype(F32))
        o_ref[bb] = hb.astype(BF16)


def _lru(p, x_col0, conv_w, conv_b, w_blk, ba, bx, lam, h0, hf=None, lg_col0=None, *, reverse):
    nb, t, _ = p.shape
    tl = min(LRU_TL, t)
    n = t // tl
    cb = LRU_CB
    xb0 = x_col0 // cb
    gated = hf is not None
    hpc = tl // HALO

    def pos(i):
        return n - 1 - i if reverse else i

    in_specs = [
        pl.BlockSpec((nb, tl, cb), lambda c, i: (0, pos(i), xb0 + c)),
        pl.BlockSpec((nb, HALO, cb), lambda c, i: (0, jnp.maximum(pos(i) * hpc - 1, 0), xb0 + c)),
        pl.BlockSpec((nb, HALO, cb),
                     lambda c, i: (0, jnp.minimum((pos(i) + 1) * hpc, t // HALO - 1), xb0 + c)),
        pl.BlockSpec((4, cb), lambda c, i: (0, c)),
        pl.BlockSpec((1, cb), lambda c, i: (0, c)),
        pl.BlockSpec((cb // LRU_BLOCK_DIM, LRU_BLOCK_DIM, 2 * LRU_BLOCK_DIM), lambda c, i: (c, 0, 0)),
        pl.BlockSpec((1, cb), lambda c, i: (0, c)),
        pl.BlockSpec((1, cb), lambda c, i: (0, c)),
        pl.BlockSpec((1, cb), lambda c, i: (0, c)),
        pl.BlockSpec((nb, cb), lambda c, i: (0, c)),
    ]
    args = [p, p, p, conv_w, conv_b.reshape(1, -1), w_blk, ba.reshape(1, -1), bx.reshape(1, -1),
            lam.reshape(1, -1), h0]
    if gated:
        lb0 = lg_col0 // cb
        in_specs += [
            pl.BlockSpec((nb, tl, cb), lambda c, i: (0, pos(i), c)),
            pl.BlockSpec((nb, tl, cb), lambda c, i: (0, pos(i), lb0 + c)),
        ]
        args += [hf, p]
    kern = functools.partial(_lru_kernel, reverse=reverse, gated=gated, nb=nb)
    return pl.pallas_call(
        kern,
        grid=(LRU_W // cb, n),
        in_specs=in_specs,
        out_specs=[
            pl.BlockSpec((nb, tl, cb), lambda c, i: (0, pos(i), c)),
            pl.BlockSpec((nb, cb), lambda c, i: (0, c)),
        ],
        out_shape=[
            jax.ShapeDtypeStruct((nb, t, LRU_W), BF16),
            jax.ShapeDtypeStruct((nb, LRU_W), F32),
        ],
        scratch_shapes=[
            pltpu.VMEM((nb, tl + 2 * HALO, cb), F32),
            pltpu.VMEM((cb // LRU_BLOCK_DIM, nb * LRU_PAD, LRU_BLOCK_DIM), F32),
            pltpu.VMEM((cb // LRU_BLOCK_DIM, nb * LRU_PAD, LRU_BLOCK_DIM), F32),
            pltpu.VMEM((cb // LRU_BLOCK_DIM, nb, LRU_BLOCK_DIM), F32),
        ],
        compiler_params=_params("parallel", "arbitrary"),
        name="lru_bwd" if reverse else "lru_fwd",
    )(*args)


MERGE_TM = 512
MERGE_TK = 1024


def _merge_kernel(x_ref, mod_ref, attn_ref, lru_ref, ga_ref, gl_ref, w_ref, o_ref, acc_scr):
    kk = pl.program_id(2)

    @pl.when(kk == 0)
    def _():
        acc_scr[...] = jnp.zeros_like(acc_scr)

    m = (jax.nn.sigmoid(ga_ref[0].astype(F32)) * attn_ref[0].astype(F32)
         + jax.nn.sigmoid(gl_ref[0].astype(F32)) * lru_ref[0].astype(F32))
    acc_scr[...] += _dot(m.astype(BF16), w_ref[...])

    @pl.when(kk == pl.num_programs(2) - 1)
    def _():
        o_ref[0] = x_ref[0] + mod_ref[0, 5:6, :] * acc_scr[...]


def _merge(x, mod, attn, lru, p, w_out):
    b, s, d = x.shape
    ga0 = OFF_GA // MERGE_TK
    gl0 = OFF_GL // MERGE_TK
    return pl.pallas_call(
        _merge_kernel,
        grid=(b, s // MERGE_TM, d // MERGE_TK),
        in_specs=[
            pl.BlockSpec((1, MERGE_TM, d), lambda bb, i, k: (bb, i, 0)),
            pl.BlockSpec((1, N_MOD, d), lambda bb, i, k: (bb, 0, 0)),
            pl.BlockSpec((1, MERGE_TM, MERGE_TK), lambda bb, i, k: (bb, i, k)),
            pl.BlockSpec((1, MERGE_TM, MERGE_TK), lambda bb, i, k: (bb, i, k)),
            pl.BlockSpec((1, MERGE_TM, MERGE_TK), lambda bb, i, k: (bb, i, ga0 + k)),
            pl.BlockSpec((1, MERGE_TM, MERGE_TK), lambda bb, i, k: (bb, i, gl0 + k)),
            pl.BlockSpec((MERGE_TK, d), lambda bb, i, k: (k, 0)),
        ],
        out_specs=pl.BlockSpec((1, MERGE_TM, d), lambda bb, i, k: (bb, i, 0)),
        out_shape=jax.ShapeDtypeStruct((b, s, d), F32),
        scratch_shapes=[pltpu.VMEM((MERGE_TM, d), F32)],
        compiler_params=_params("parallel", "parallel", "arbitrary"),
        name="merge",
    )(x, mod, attn, lru, p, p, w_out)


def _deinterleave_heads(w, n_heads):
    lead = w.shape[:-1]
    w = w.reshape(lead + (n_heads, HEAD_DIM // 2, 2))
    w = jnp.swapaxes(w, -1, -2)
    return w.reshape(lead + (n_heads * HEAD_DIM,))


def _rope_tables(n_tok):
    rows = n_tok // GRID_W
    row = jnp.repeat(jnp.arange(rows, dtype=F32), GRID_W)
    col = jnp.tile(jnp.arange(GRID_W, dtype=F32), rows)
    axis_dims = HEAD_DIM // 2
    freqs = ROPE_THETA ** (-jnp.arange(0, axis_dims, 2, dtype=F32) / axis_dims)
    ang = jnp.concatenate([row[:, None] * freqs, col[:, None] * freqs], axis=-1)
    cos, sin = jnp.cos(ang), jnp.sin(ang)
    return jnp.concatenate([cos, cos], axis=-1), jnp.concatenate([-sin, sin], axis=-1)


def kernel(x, c, ctx, c_ctx, w_mod, b_mod, norm_g, ffn_wg, ffn_wu, ffn_wd, w_in, w_out, q_norm_g,
           k_norm_g, conv_w, conv_b, lru_wa, lru_ba, lru_wx, lru_bx, lru_lambda, final_norm_g):
    b, s, d = x.shape
    n_ctx = ctx.shape[1]
    ctx_row = b

    cc = jnp.concatenate([c, c_ctx[None], jnp.zeros((SUBLANES - b - 1, d), F32)], axis=0)
    mod = _modulation(cc, w_mod[0], b_mod[0]).reshape(SUBLANES, N_MOD, d)

    wg = ffn_wg[0].astype(BF16)
    wu = ffn_wu[0].astype(BF16)
    wd = ffn_wd[0].astype(BF16)
    w_in_p = jnp.concatenate([
        _deinterleave_heads(w_in[0][:, OFF_Q:OFF_K], N_Q_HEADS),
        _deinterleave_heads(w_in[0][:, OFF_K:OFF_V], N_KV_HEADS),
        w_in[0][:, OFF_V:],
    ], axis=1).astype(BF16)
    w_out_b = w_out[0].astype(BF16)
    qkg = jnp.stack([
        _deinterleave_heads(q_norm_g[0], 1) * (HEAD_DIM ** -0.5),
        _deinterleave_heads(k_norm_g[0], 1),
    ])
    cosf, sinf = _rope_tables(s)
    w_lru = jnp.concatenate([lru_wa[0], lru_wx[0]], axis=-1).astype(BF16)

    lat_row = lambda bb: bb
    ctx_rowf = lambda bb: ctx_row

    x1 = _ffn(x, mod, lat_row, norm_g[0, 0], wg, wu, wd, final_norm_g,
              ffn_idx=0, mod_base=0, final_norm=False, tm=512)
    ctx_flat = ctx.reshape(1, b * n_ctx, d)
    ctx1 = _ffn(ctx_flat, mod, ctx_rowf, norm_g[0, 0], wg, wu, wd, final_norm_g,
                ffn_idx=0, mod_base=0, final_norm=False, tm=512)

    pc = _proj(ctx1, mod, ctx_rowf, norm_g[0, 1], w_in_p, qkg, cosf, sinf,
               col0=PROJ_TILE_KV, ncol=3, rope=False, tm=b * n_ctx)
    pc = pc.reshape(b, n_ctx, 3 * PROJ_TN)
    zeros = jnp.zeros((b, LRU_W), F32)
    lx_c = OFF_LX - OFF_K
    _, hf0 = _lru(pc, lx_c, conv_w[0], conv_b[0], w_lru[0], lru_ba[0, 0], lru_bx[0, 0],
                  lru_lambda[0, 0], zeros, reverse=False)
    _, hb0 = _lru(pc, lx_c, conv_w[0], conv_b[0], w_lru[1], lru_ba[0, 1], lru_bx[0, 1],
                  lru_lambda[0, 1], zeros, reverse=True)

    p = _proj(x1, mod, lat_row, norm_g[0, 1], w_in_p, qkg, cosf, sinf,
              col0=0, ncol=IN_W // PROJ_TN, rope=True, tm=1024)
    attn = _attention(p, pc)
    hf, _ = _lru(p, OFF_LX, conv_w[0], conv_b[0], w_lru[0], lru_ba[0, 0], lru_bx[0, 0],
                 lru_lambda[0, 0], hf0, reverse=False)
    lru, _ = _lru(p, OFF_LX, conv_w[0], conv_b[0], w_lru[1], lru_ba[0, 1], lru_bx[0, 1],
                  lru_lambda[0, 1], hb0, hf=hf, lg_col0=OFF_LG, reverse=True)
    x2 = _merge(x1, mod, attn, lru, p, w_out_b)

    return _ffn(x2, mod, lat_row, norm_g[0, 2], wg, wu, wd, final_norm_g,
                ffn_idx=1, mod_base=6, final_norm=True, tm=512)
```

```python
import functools

import jax
import jax.numpy as jnp
from jax import lax
from jax.experimental import pallas as pl
from jax.experimental.pallas import tpu as pltpu

D_MODEL = 2048
GRID_W = 64
HEAD_DIM = 128
N_Q_HEADS = 16
N_KV_HEADS = 4
GQA_GROUP = N_Q_HEADS // N_KV_HEADS
ATTN_W = N_Q_HEADS * HEAD_DIM
KV_W = N_KV_HEADS * HEAD_DIM
LRU_W = D_MODEL
LRU_BLOCK_DIM = 128
LRU_C = 8.0
D_FF = 5632
ROPE_THETA = 10000.0
EPS = 1e-6
N_MOD = 9
FFN_RES = 0.5
LOG2_E = 1.4426950408889634
OFF_Q = 0
OFF_K = OFF_Q + ATTN_W
OFF_V = OFF_K + KV_W
OFF_LX = OFF_V + KV_W
OFF_LG = OFF_LX + LRU_W
OFF_GA = OFF_LG + LRU_W
OFF_GL = OFF_GA + D_MODEL
IN_W = OFF_GL + D_MODEL

SUBLANES = 8
LANES = 128
VMEM_LIMIT = 56 * 1024 * 1024

BF16 = jnp.bfloat16
F32 = jnp.float32


def _params(*sem):
    return pltpu.CompilerParams(dimension_semantics=sem, vmem_limit_bytes=VMEM_LIMIT)


def _dot(a, b):
    return jnp.dot(a, b, preferred_element_type=F32)


def _rms(x):
    return x * lax.rsqrt(jnp.mean(x * x, axis=-1, keepdims=True) + EPS)


MOD_TN = 1024


def _mod_kernel(c_ref, w_ref, b_ref, o_ref):
    c = c_ref[...]
    s = c * jax.nn.sigmoid(c)
    o_ref[...] = jnp.dot(s, w_ref[...], preferred_element_type=F32,
                         precision=lax.Precision.HIGHEST) + b_ref[...]


def _modulation(cc, w_mod, b_mod):
    rows, d = cc.shape
    n = w_mod.shape[1]
    return pl.pallas_call(
        _mod_kernel,
        grid=(n // MOD_TN,),
        in_specs=[
            pl.BlockSpec((rows, d), lambda j: (0, 0)),
            pl.BlockSpec((d, MOD_TN), lambda j: (0, j)),
            pl.BlockSpec((1, MOD_TN), lambda j: (0, j)),
        ],
        out_specs=pl.BlockSpec((rows, MOD_TN), lambda j: (0, j)),
        out_shape=jax.ShapeDtypeStruct((rows, n), F32),
        compiler_params=_params("arbitrary"),
        name="modulation",
    )(cc, w_mod, b_mod.reshape(1, n))


FFN_TF = 512


def _ffn_kernel(x_ref, mod_ref, g_ref, wg_ref, wu_ref, wd_ref, fg_ref, o_ref, h_scr, acc_scr,
                *, mod_base, final_norm):
    j = pl.program_id(2)

    @pl.when(j == 0)
    def _():
        y = _rms(x_ref[0]) * g_ref[...]
        shift = mod_ref[0, mod_base:mod_base + 1, :]
        scale = mod_ref[0, mod_base + 1:mod_base + 2, :]
        h_scr[...] = (y * (1.0 + scale) + shift).astype(BF16)
        acc_scr[...] = jnp.zeros_like(acc_scr)

    h = h_scr[...]
    gate = _dot(h, wg_ref[...])
    up = _dot(h, wu_ref[...])
    act = (gate * jax.nn.sigmoid(gate) * up).astype(BF16)
    acc_scr[...] += _dot(act, wd_ref[...])

    @pl.when(j == pl.num_programs(2) - 1)
    def _():
        g = mod_ref[0, mod_base + 2:mod_base + 3, :]
        xo = x_ref[0] + FFN_RES * g * acc_scr[...]
        if final_norm:
            xo = _rms(xo) * fg_ref[...]
        o_ref[0] = xo


def _ffn(x, mod, mod_row, norm_g, wg, wu, wd, final_g, *, ffn_idx, mod_base, final_norm, tm):
    b, s, d = x.shape
    kern = functools.partial(_ffn_kernel, mod_base=mod_base, final_norm=final_norm)
    return pl.pallas_call(
        kern,
        grid=(b, s // tm, D_FF // FFN_TF),
        in_specs=[
            pl.BlockSpec((1, tm, d), lambda bb, i, j: (bb, i, 0)),
            pl.BlockSpec((1, N_MOD, d), lambda bb, i, j: (mod_row(bb), 0, 0)),
            pl.BlockSpec((1, d), lambda bb, i, j: (0, 0)),
            pl.BlockSpec((None, d, FFN_TF), lambda bb, i, j: (ffn_idx, 0, j)),
            pl.BlockSpec((None, d, FFN_TF), lambda bb, i, j: (ffn_idx, 0, j)),
            pl.BlockSpec((None, FFN_TF, d), lambda bb, i, j: (ffn_idx, j, 0)),
            pl.BlockSpec((1, d), lambda bb, i, j: (0, 0)),
        ],
        out_specs=pl.BlockSpec((1, tm, d), lambda bb, i, j: (bb, i, 0)),
        out_shape=jax.ShapeDtypeStruct((b, s, d), F32),
        scratch_shapes=[pltpu.VMEM((tm, d), BF16), pltpu.VMEM((tm, d), F32)],
        compiler_params=_params("parallel", "parallel", "arbitrary"),
        name="ffn",
    )(x, mod, norm_g.reshape(1, d), wg, wu, wd, final_g.reshape(1, d))


PROJ_TN = 1024
PROJ_TILE_KV = OFF_K // PROJ_TN


def _proj_kernel(x_ref, mod_ref, g_ref, w_ref, qkg_ref, cos_ref, sin_ref, o_ref, h_scr,
                 *, mod_base, col0, rope):
    jj = pl.program_id(2)
    j = jj + col0

    @pl.when(jj == 0)
    def _():
        y = _rms(x_ref[0]) * g_ref[...]
        shift = mod_ref[0, mod_base:mod_base + 1, :]
        scale = mod_ref[0, mod_base + 1:mod_base + 2, :]
        h_scr[...] = (y * (1.0 + scale) + shift).astype(BF16)

    z = _dot(h_scr[...], w_ref[...])

    def head_norm(t, gain):
        y = _rms(t) * gain
        if rope:
            y = y * cos_ref[...] + pltpu.roll(y, HEAD_DIM // 2, axis=1) * sin_ref[...]
        return y.astype(BF16)

    def store_heads(n_heads, gain):
        for hh in range(n_heads):
            sl = slice(hh * HEAD_DIM, (hh + 1) * HEAD_DIM)
            o_ref[0, :, sl] = head_norm(z[:, sl], gain)

    @pl.when(j < PROJ_TILE_KV)
    def _():
        store_heads(PROJ_TN // HEAD_DIM, qkg_ref[0:1, :])

    @pl.when(j == PROJ_TILE_KV)
    def _():
        store_heads(N_KV_HEADS, qkg_ref[1:2, :])
        o_ref[0, :, KV_W:] = z[:, KV_W:].astype(BF16)

    @pl.when(j > PROJ_TILE_KV)
    def _():
        o_ref[0] = z.astype(BF16)


def _proj(x, mod, mod_row, norm_g, w_in, qkg, cosf, sinf, *, col0, ncol, rope, tm):
    b, s, d = x.shape
    kern = functools.partial(_proj_kernel, mod_base=3, col0=col0, rope=rope)
    return pl.pallas_call(
        kern,
        grid=(b, s // tm, ncol),
        in_specs=[
            pl.BlockSpec((1, tm, d), lambda bb, i, j: (bb, i, 0)),
            pl.BlockSpec((1, N_MOD, d), lambda bb, i, j: (mod_row(bb), 0, 0)),
            pl.BlockSpec((1, d), lambda bb, i, j: (0, 0)),
            pl.BlockSpec((d, PROJ_TN), lambda bb, i, j: (0, j + col0)),
            pl.BlockSpec((2, HEAD_DIM), lambda bb, i, j: (0, 0)),
            pl.BlockSpec((tm, HEAD_DIM), lambda bb, i, j: (i, 0)),
            pl.BlockSpec((tm, HEAD_DIM), lambda bb, i, j: (i, 0)),
        ],
        out_specs=pl.BlockSpec((1, tm, PROJ_TN), lambda bb, i, j: (bb, i, j)),
        out_shape=jax.ShapeDtypeStruct((b, s, ncol * PROJ_TN), BF16),
        scratch_shapes=[pltpu.VMEM((tm, d), BF16)],
        compiler_params=_params("parallel", "parallel", "arbitrary"),
        name="proj",
    )(x, mod, norm_g.reshape(1, d), w_in, qkg, cosf, sinf)


ATTN_TQ = 256


def _attn_kernel(q_ref, qn_ref, k_ref, v_ref, kc_ref, vc_ref, o_ref,
                 vt_scr, vct_scr, s_scr, sc_scr, m_scr):
    i = pl.program_id(2)
    tq = q_ref.shape[1]
    k = k_ref[0]
    kc = kc_ref[0]
    nt = (((1,), (1,)), ((), ()))
    pair_w = 2 * HEAD_DIM

    def stack_pair(ref, col0):
        return jnp.concatenate([ref[0, :, col0:col0 + HEAD_DIM],
                                ref[0, :, col0 + HEAD_DIM:col0 + pair_w]], axis=0)

    def scores(q2):
        s1 = lax.dot_general(k, q2, nt, preferred_element_type=F32)
        s2 = lax.dot_general(kc, q2, nt, preferred_element_type=F32)
        m = jnp.maximum(jnp.max(s1, axis=0, keepdims=True), jnp.max(s2, axis=0, keepdims=True))
        return s1, s2, m

    def finish(s1, s2, m, col0):
        p1 = jnp.exp2(s1 - m)
        p2 = jnp.exp2(s2 - m)
        l = jnp.sum(p1, axis=0, keepdims=True) + jnp.sum(p2, axis=0, keepdims=True)
        ot = _dot(vt_scr[...], p1.astype(BF16)) + _dot(vct_scr[...], p2.astype(BF16))
        ot = ot * (1.0 / l)
        for e in range(2):
            o_ref[0, :, col0 + e * HEAD_DIM:col0 + (e + 1) * HEAD_DIM] = (
                ot[:, e * tq:(e + 1) * tq].T.astype(BF16))

    @pl.when(i == 0)
    def _():
        vt_scr[...] = v_ref[0].T
        vct_scr[...] = vc_ref[0].T
        s_scr[...], sc_scr[...], m_scr[...] = scores(stack_pair(q_ref, 0))

    s1b, s2b, mb = scores(stack_pair(q_ref, pair_w))
    finish(s_scr[...], sc_scr[...], m_scr[...], 0)
    s1n, s2n, mn = scores(stack_pair(qn_ref, 0))
    finish(s1b, s2b, mb, pair_w)
    s_scr[...] = s1n
    sc_scr[...] = s2n
    m_scr[...] = mn


def _attention(p, pc):
    b, s, _ = p.shape
    c = pc.shape[1]
    gw = GQA_GROUP * HEAD_DIM
    kb = OFF_K // HEAD_DIM
    vb = OFF_V // HEAD_DIM
    n = s // ATTN_TQ
    return pl.pallas_call(
        _attn_kernel,
        grid=(b, N_KV_HEADS, n),
        in_specs=[
            pl.BlockSpec((1, ATTN_TQ, gw), lambda bb, g, i: (bb, i, g)),
            pl.BlockSpec((1, ATTN_TQ, 2 * HEAD_DIM),
                         lambda bb, g, i: (bb, jnp.minimum(i + 1, n - 1), 2 * g)),
            pl.BlockSpec((1, s, HEAD_DIM), lambda bb, g, i: (bb, 0, kb + g)),
            pl.BlockSpec((1, s, HEAD_DIM), lambda bb, g, i: (bb, 0, vb + g)),
            pl.BlockSpec((1, c, HEAD_DIM), lambda bb, g, i: (bb, 0, g)),
            pl.BlockSpec((1, c, HEAD_DIM), lambda bb, g, i: (bb, 0, N_KV_HEADS + g)),
        ],
        out_specs=pl.BlockSpec((1, ATTN_TQ, gw), lambda bb, g, i: (bb, i, g)),
        out_shape=jax.ShapeDtypeStruct((b, s, ATTN_W), BF16),
        scratch_shapes=[
            pltpu.VMEM((HEAD_DIM, s), BF16),
            pltpu.VMEM((HEAD_DIM, c), BF16),
            pltpu.VMEM((s, 2 * ATTN_TQ), F32),
            pltpu.VMEM((c, 2 * ATTN_TQ), F32),
            pltpu.VMEM((1, 2 * ATTN_TQ), F32),
        ],
        compiler_params=_params("arbitrary", "arbitrary", "arbitrary"),
        name="attention",
    )(p, p, p, p, pc, pc)


LRU_CB = 512
LRU_TL = 256
LRU_PAD = LRU_TL + SUBLANES
HALO = SUBLANES
CONV_LEFT = 2


def _gelu_tanh(x):
    return 0.5 * x * (1.0 + jnp.tanh(0.7978845608028654 * (x + 0.044715 * x * x * x)))


def _lru_kernel(*refs, reverse, gated, nb):
    if gated:
        (x_ref, xp_ref, xn_ref, cw_ref, cb_ref, w_ref, ba_ref, bx_ref, lam_ref, h0_ref,
         hf_ref, lg_ref, o_ref, hl_ref, a_scr, u_scr, ho_scr, h_scr) = refs
    else:
        (x_ref, xp_ref, xn_ref, cw_ref, cb_ref, w_ref, ba_ref, bx_ref, lam_ref, h0_ref,
         o_ref, hl_ref, a_scr, u_scr, ho_scr, h_scr) = refs
    i = pl.program_id(1)
    n = pl.num_programs(1)
    ci = n - 1 - i if reverse else i
    tl = x_ref.shape[1]
    cb = x_ref.shape[2]

    @pl.when(i == 0)
    def _():
        for blk in range(cb // LRU_BLOCK_DIM):
            h_scr[blk] = h0_ref[:, blk * LRU_BLOCK_DIM:(blk + 1) * LRU_BLOCK_DIM]

    ri = lax.broadcasted_iota(jnp.int32, (tl, tl), 0)
    cj = lax.broadcasted_iota(jnp.int32, (tl, tl), 1)
    shifts = jnp.concatenate(
        [(cj == ri + d).astype(BF16) for d in (-2, -1, 1)], axis=0)
    row8 = lax.broadcasted_iota(jnp.int32, (HALO, cb), 0)
    has_prev = (ci > 0).astype(F32)
    has_next = (ci < n - 1).astype(F32)
    w0, w1, w2, w3 = (cw_ref[k:k + 1, :] for k in range(4))
    xcs = []
    for bb in range(nb):
        xb = x_ref[bb]
        sh = _dot(shifts, xb)
        y = (cb_ref[...] + w2 * xb.astype(F32) + w0 * sh[:tl] + w1 * sh[tl:2 * tl]
             + w3 * sh[2 * tl:])
        prev = xp_ref[bb].astype(F32) * has_prev
        nxt = xn_ref[bb].astype(F32) * has_next
        head = (w0 * jnp.where(row8 < 2, pltpu.roll(prev, 2, axis=0), 0.0)
                + w1 * jnp.where(row8 < 1, pltpu.roll(prev, 1, axis=0), 0.0))
        tail = w3 * jnp.where(row8 == HALO - 1, pltpu.roll(nxt, HALO - 1, axis=0), 0.0)
        xcs.append(jnp.concatenate(
            [y[:HALO] + head, y[HALO:tl - HALO], y[tl - HALO:] + tail], axis=0))
    xc = jnp.concatenate(xcs, axis=0)

    lam = lam_ref[...]
    neg = -lam
    sp = jnp.maximum(neg, 0.0) + jnp.log1p(jnp.exp(-jnp.abs(neg)))
    half_c = (-0.5 * LRU_C * LOG2_E) * sp
    for blk in range(cb // LRU_BLOCK_DIM):
        sl = slice(blk * LRU_BLOCK_DIM, (blk + 1) * LRU_BLOCK_DIM)
        xb = xc[:, sl]
        z = _dot(xb.astype(BF16), w_ref[blk])
        tr = jnp.tanh(z[:, :LRU_BLOCK_DIM] + ba_ref[:, sl])
        ti = jnp.tanh(z[:, LRU_BLOCK_DIM:] + bx_ref[:, sl])
        hc = half_c[:, sl]
        a = jnp.exp2(hc + hc * tr)
        hx = 0.5 * xb
        y = 1.0 - a * a
        root = jnp.where(y > 0.0, y * lax.rsqrt(y), 0.0)
        u = root * (hx + hx * ti)
        for bb in range(nb):
            a_scr[blk, bb * LRU_PAD:bb * LRU_PAD + tl, :] = a[bb * tl:(bb + 1) * tl]
            u_scr[blk, bb * LRU_PAD:bb * LRU_PAD + tl, :] = u[bb * tl:(bb + 1) * tl]

    nblk = cb // LRU_BLOCK_DIM

    def step(t, hs):
        tt = tl - 1 - t if reverse else t
        rows = pl.ds(tt, nb, stride=LRU_PAD)
        out = []
        for blk in range(nblk):
            h = a_scr[blk, rows, :] * hs[blk] + u_scr[blk, rows, :]
            ho_scr[blk, rows, :] = h
            out.append(h)
        return tuple(out)

    hs = lax.fori_loop(0, tl, step, tuple(h_scr[blk] for blk in range(nblk)), unroll=8)
    for blk in range(nblk):
        h_scr[blk] = hs[blk]
        hl_ref[:, blk * LRU_BLOCK_DIM:(blk + 1) * LRU_BLOCK_DIM] = hs[blk]

    for bb in range(nb):
        hb = jnp.concatenate(
            [ho_scr[blk, bb * LRU_PAD:bb * LRU_PAD + tl, :] for blk in range(nblk)], axis=-1)
        if gated:
            hb = (hb + hf_ref[bb].astype(F32)) * _gelu_tanh(lg_ref[bb].astype(F32))
        o_ref[bb] = hb.astype(BF16)


def _lru(p, x_col0, conv_w, conv_b, w_blk, ba, bx, lam, h0, hf=None, lg_col0=None, *, reverse):
    nb, t, _ = p.shape
    tl = min(LRU_TL, t)
    n = t // tl
    cb = LRU_CB
    xb0 = x_col0 // cb
    gated = hf is not None
    hpc = tl // HALO

    def pos(i):
        return n - 1 - i if reverse else i

    in_specs = [
        pl.BlockSpec((nb, tl, cb), lambda c, i: (0, pos(i), xb0 + c)),
        pl.BlockSpec((nb, HALO, cb), lambda c, i: (0, jnp.maximum(pos(i) * hpc - 1, 0), xb0 + c)),
        pl.BlockSpec((nb, HALO, cb),
                     lambda c, i: (0, jnp.minimum((pos(i) + 1) * hpc, t // HALO - 1), xb0 + c)),
        pl.BlockSpec((4, cb), lambda c, i: (0, c)),
        pl.BlockSpec((1, cb), lambda c, i: (0, c)),
        pl.BlockSpec((cb // LRU_BLOCK_DIM, LRU_BLOCK_DIM, 2 * LRU_BLOCK_DIM), lambda c, i: (c, 0, 0)),
        pl.BlockSpec((1, cb), lambda c, i: (0, c)),
        pl.BlockSpec((1, cb), lambda c, i: (0, c)),
        pl.BlockSpec((1, cb), lambda c, i: (0, c)),
        pl.BlockSpec((nb, cb), lambda c, i: (0, c)),
    ]
    args = [p, p, p, conv_w, conv_b.reshape(1, -1), w_blk, ba.reshape(1, -1), bx.reshape(1, -1),
            lam.reshape(1, -1), h0]
    if gated:
        lb0 = lg_col0 // cb
        in_specs += [
            pl.BlockSpec((nb, tl, cb), lambda c, i: (0, pos(i), c)),
            pl.BlockSpec((nb, tl, cb), lambda c, i: (0, pos(i), lb0 + c)),
        ]
        args += [hf, p]
    kern = functools.partial(_lru_kernel, reverse=reverse, gated=gated, nb=nb)
    return pl.pallas_call(
        kern,
        grid=(LRU_W // cb, n),
        in_specs=in_specs,
        out_specs=[
            pl.BlockSpec((nb, tl, cb), lambda c, i: (0, pos(i), c)),
            pl.BlockSpec((nb, cb), lambda c, i: (0, c)),
        ],
        out_shape=[
            jax.ShapeDtypeStruct((nb, t, LRU_W), BF16),
            jax.ShapeDtypeStruct((nb, LRU_W), F32),
        ],
        scratch_shapes=[
            pltpu.VMEM((cb // LRU_BLOCK_DIM, nb * LRU_PAD, LRU_BLOCK_DIM), F32),
            pltpu.VMEM((cb // LRU_BLOCK_DIM, nb * LRU_PAD, LRU_BLOCK_DIM), F32),
            pltpu.VMEM((cb // LRU_BLOCK_DIM, nb * LRU_PAD, LRU_BLOCK_DIM), F32),
            pltpu.VMEM((cb // LRU_BLOCK_DIM, nb, LRU_BLOCK_DIM), F32),
        ],
        compiler_params=_params("parallel", "arbitrary"),
        name="lru_bwd" if reverse else "lru_fwd",
    )(*args)


MERGE_TM = 512
MERGE_TK = 1024


def _merge_kernel(x_ref, mod_ref, attn_ref, lru_ref, ga_ref, gl_ref, w_ref, o_ref, acc_scr):
    kk = pl.program_id(2)

    @pl.when(kk == 0)
    def _():
        acc_scr[...] = jnp.zeros_like(acc_scr)

    m = (jax.nn.sigmoid(ga_ref[0].astype(F32)) * attn_ref[0].astype(F32)
         + jax.nn.sigmoid(gl_ref[0].astype(F32)) * lru_ref[0].astype(F32))
    acc_scr[...] += _dot(m.astype(BF16), w_ref[...])

    @pl.when(kk == pl.num_programs(2) - 1)
    def _():
        o_ref[0] = x_ref[0] + mod_ref[0, 5:6, :] * acc_scr[...]


def _merge(x, mod, attn, lru, p, w_out):
    b, s, d = x.shape
    ga0 = OFF_GA // MERGE_TK
    gl0 = OFF_GL // MERGE_TK
    return pl.pallas_call(
        _merge_kernel,
        grid=(b, s // MERGE_TM, d // MERGE_TK),
        in_specs=[
            pl.BlockSpec((1, MERGE_TM, d), lambda bb, i, k: (bb, i, 0)),
            pl.BlockSpec((1, N_MOD, d), lambda bb, i, k: (bb, 0, 0)),
            pl.BlockSpec((1, MERGE_TM, MERGE_TK), lambda bb, i, k: (bb, i, k)),
            pl.BlockSpec((1, MERGE_TM, MERGE_TK), lambda bb, i, k: (bb, i, k)),
            pl.BlockSpec((1, MERGE_TM, MERGE_TK), lambda bb, i, k: (bb, i, ga0 + k)),
            pl.BlockSpec((1, MERGE_TM, MERGE_TK), lambda bb, i, k: (bb, i, gl0 + k)),
            pl.BlockSpec((MERGE_TK, d), lambda bb, i, k: (k, 0)),
        ],
        out_specs=pl.BlockSpec((1, MERGE_TM, d), lambda bb, i, k: (bb, i, 0)),
        out_shape=jax.ShapeDtypeStruct((b, s, d), F32),
        scratch_shapes=[pltpu.VMEM((MERGE_TM, d), F32)],
        compiler_params=_params("parallel", "parallel", "arbitrary"),
        name="merge",
    )(x, mod, attn, lru, p, p, w_out)


def _deinterleave_heads(w, n_heads):
    lead = w.shape[:-1]
    w = w.reshape(lead + (n_heads, HEAD_DIM // 2, 2))
    w = jnp.swapaxes(w, -1, -2)
    return w.reshape(lead + (n_heads * HEAD_DIM,))


def _rope_tables(n_tok):
    rows = n_tok // GRID_W
    row = jnp.repeat(jnp.arange(rows, dtype=F32), GRID_W)
    col = jnp.tile(jnp.arange(GRID_W, dtype=F32), rows)
    axis_dims = HEAD_DIM // 2
    freqs = ROPE_THETA ** (-jnp.arange(0, axis_dims, 2, dtype=F32) / axis_dims)
    ang = jnp.concatenate([row[:, None] * freqs, col[:, None] * freqs], axis=-1)
    cos, sin = jnp.cos(ang), jnp.sin(ang)
    return jnp.concatenate([cos, cos], axis=-1), jnp.concatenate([-sin, sin], axis=-1)


def kernel(x, c, ctx, c_ctx, w_mod, b_mod, norm_g, ffn_wg, ffn_wu, ffn_wd, w_in, w_out, q_norm_g,
           k_norm_g, conv_w, conv_b, lru_wa, lru_ba, lru_wx, lru_bx, lru_lambda, final_norm_g):
    b, s, d = x.shape
    n_ctx = ctx.shape[1]
    ctx_row = b

    cc = jnp.concatenate([c, c_ctx[None], jnp.zeros((SUBLANES - b - 1, d), F32)], axis=0)
    mod = _modulation(cc, w_mod[0], b_mod[0]).reshape(SUBLANES, N_MOD, d)

    wg = ffn_wg[0].astype(BF16)
    wu = ffn_wu[0].astype(BF16)
    wd = ffn_wd[0].astype(BF16)
    w_in_p = jnp.concatenate([
        _deinterleave_heads(w_in[0][:, OFF_Q:OFF_K], N_Q_HEADS),
        _deinterleave_heads(w_in[0][:, OFF_K:OFF_V], N_KV_HEADS),
        w_in[0][:, OFF_V:],
    ], axis=1).astype(BF16)
    w_out_b = w_out[0].astype(BF16)
    qkg = jnp.stack([
        _deinterleave_heads(q_norm_g[0], 1) * (HEAD_DIM ** -0.5 * LOG2_E),
        _deinterleave_heads(k_norm_g[0], 1),
    ])
    cosf, sinf = _rope_tables(s)
    w_lru = (0.5 * jnp.concatenate([lru_wa[0], lru_wx[0]], axis=-1)).astype(BF16)
    ba_h = 0.5 * lru_ba[0]
    bx_h = 0.5 * lru_bx[0]

    lat_row = lambda bb: bb
    ctx_rowf = lambda bb: ctx_row

    x1 = _ffn(x, mod, lat_row, norm_g[0, 0], wg, wu, wd, final_norm_g,
              ffn_idx=0, mod_base=0, final_norm=False, tm=512)
    ctx_flat = ctx.reshape(1, b * n_ctx, d)
    ctx1 = _ffn(ctx_flat, mod, ctx_rowf, norm_g[0, 0], wg, wu, wd, final_norm_g,
                ffn_idx=0, mod_base=0, final_norm=False, tm=512)

    pc = _proj(ctx1, mod, ctx_rowf, norm_g[0, 1], w_in_p, qkg, cosf, sinf,
               col0=PROJ_TILE_KV, ncol=3, rope=False, tm=b * n_ctx)
    pc = pc.reshape(b, n_ctx, 3 * PROJ_TN)
    zeros = jnp.zeros((b, LRU_W), F32)
    lx_c = OFF_LX - OFF_K
    _, hf0 = _lru(pc, lx_c, conv_w[0], conv_b[0], w_lru[0], ba_h[0], bx_h[0],
                  lru_lambda[0, 0], zeros, reverse=False)
    _, hb0 = _lru(pc, lx_c, conv_w[0], conv_b[0], w_lru[1], ba_h[1], bx_h[1],
                  lru_lambda[0, 1], zeros, reverse=True)

    p = _proj(x1, mod, lat_row, norm_g[0, 1], w_in_p, qkg, cosf, sinf,
              col0=0, ncol=IN_W // PROJ_TN, rope=True, tm=1024)
    attn = _attention(p, pc)
    hf, _ = _lru(p, OFF_LX, conv_w[0], conv_b[0], w_lru[0], ba_h[0], bx_h[0],
                 lru_lambda[0, 0], hf0, reverse=False)
    lru, _ = _lru(p, OFF_LX, conv_w[0], conv_b[0], w_lru[1], ba_h[1], bx_h[1],
                  lru_lambda[0, 1], hb0, hf=hf, lg_col0=OFF_LG, reverse=True)
    x2 = _merge(x1, mod, attn, lru, p, w_out_b)

    return _ffn(x2, mod, lat_row, norm_g[0, 2], wg, wu, wd, final_norm_g,
                ffn_idx=1, mod_base=6, final_norm=True, tm=512)
```

```python
import functools

import jax
import jax.numpy as jnp
from jax import lax
from jax.experimental import pallas as pl
from jax.experimental.pallas import tpu as pltpu

D_MODEL = 2048
GRID_W = 64
HEAD_DIM = 128
N_Q_HEADS = 16
N_KV_HEADS = 4
GQA_GROUP = N_Q_HEADS // N_KV_HEADS
ATTN_W = N_Q_HEADS * HEAD_DIM
KV_W = N_KV_HEADS * HEAD_DIM
LRU_W = D_MODEL
LRU_BLOCK_DIM = 128
LRU_C = 8.0
D_FF = 5632
ROPE_THETA = 10000.0
EPS = 1e-6
N_MOD = 9
FFN_RES = 0.5
LOG2_E = 1.4426950408889634
OFF_Q = 0
OFF_K = OFF_Q + ATTN_W
OFF_V = OFF_K + KV_W
OFF_LX = OFF_V + KV_W
OFF_LG = OFF_LX + LRU_W
OFF_GA = OFF_LG + LRU_W
OFF_GL = OFF_GA + D_MODEL
IN_W = OFF_GL + D_MODEL

SUBLANES = 8
LANES = 128
VMEM_LIMIT = 56 * 1024 * 1024

BF16 = jnp.bfloat16
F32 = jnp.float32


def _params(*sem):
    return pltpu.CompilerParams(dimension_semantics=sem, vmem_limit_bytes=VMEM_LIMIT)


def _dot(a, b):
    return jnp.dot(a, b, preferred_element_type=F32)


def _rms(x):
    return x * lax.rsqrt(jnp.mean(x * x, axis=-1, keepdims=True) + EPS)


ROW_CHUNK = 16


def _for_row_chunks(n_rows, body):
    def step(r, carry):
        body(pl.ds(pl.multiple_of(r * ROW_CHUNK, ROW_CHUNK), ROW_CHUNK))
        return carry
    lax.fori_loop(0, n_rows // ROW_CHUNK, step, 0, unroll=8)


def _norm_modulate(x_ref, g_ref, mod_ref, mod_base, h_scr):
    shift = mod_ref[0, mod_base:mod_base + 1, :]
    gain = g_ref[...] * (1.0 + mod_ref[0, mod_base + 1:mod_base + 2, :])

    def body(rows):
        x = x_ref[0, rows, :]
        rs = lax.rsqrt(jnp.mean(x * x, axis=-1, keepdims=True) + EPS)
        h_scr[rows, :] = (x * rs * gain + shift).astype(BF16)

    _for_row_chunks(h_scr.shape[0], body)


MOD_TN = 1024


def _mod_kernel(c_ref, w_ref, b_ref, o_ref):
    c = c_ref[...]
    s = c * jax.nn.sigmoid(c)
    o_ref[...] = jnp.dot(s, w_ref[...], preferred_element_type=F32,
                         precision=lax.Precision.HIGHEST) + b_ref[...]


def _modulation(cc, w_mod, b_mod):
    rows, d = cc.shape
    n = w_mod.shape[1]
    return pl.pallas_call(
        _mod_kernel,
        grid=(n // MOD_TN,),
        in_specs=[
            pl.BlockSpec((rows, d), lambda j: (0, 0)),
            pl.BlockSpec((d, MOD_TN), lambda j: (0, j)),
            pl.BlockSpec((1, MOD_TN), lambda j: (0, j)),
        ],
        out_specs=pl.BlockSpec((rows, MOD_TN), lambda j: (0, j)),
        out_shape=jax.ShapeDtypeStruct((rows, n), F32),
        compiler_params=_params("arbitrary"),
        name="modulation",
    )(cc, w_mod, b_mod.reshape(1, n))


FFN_TF = 512


def _ffn_kernel(x_ref, mod_ref, g_ref, wg_ref, wu_ref, wd_ref, fg_ref, o_ref, h_scr, acc_scr,
                *, mod_base, final_norm):
    j = pl.program_id(2)

    @pl.when(j == 0)
    def _():
        _norm_modulate(x_ref, g_ref, mod_ref, mod_base, h_scr)
        acc_scr[...] = jnp.zeros_like(acc_scr)

    h = h_scr[...]
    gate = _dot(h, wg_ref[...])
    up = _dot(h, wu_ref[...])
    act = (gate * jax.nn.sigmoid(gate) * up).astype(BF16)
    acc_scr[...] += _dot(act, wd_ref[...])

    @pl.when(j == pl.num_programs(2) - 1)
    def _():
        res_gate = FFN_RES * mod_ref[0, mod_base + 2:mod_base + 3, :]

        def body(rows):
            xo = x_ref[0, rows, :] + res_gate * acc_scr[rows, :]
            if final_norm:
                xo = _rms(xo) * fg_ref[...]
            o_ref[0, rows, :] = xo

        _for_row_chunks(acc_scr.shape[0], body)


def _ffn(x, mod, mod_row, norm_g, wg, wu, wd, final_g, *, ffn_idx, mod_base, final_norm, tm):
    b, s, d = x.shape
    kern = functools.partial(_ffn_kernel, mod_base=mod_base, final_norm=final_norm)
    return pl.pallas_call(
        kern,
        grid=(b, s // tm, D_FF // FFN_TF),
        in_specs=[
            pl.BlockSpec((1, tm, d), lambda bb, i, j: (bb, i, 0)),
            pl.BlockSpec((1, N_MOD, d), lambda bb, i, j: (mod_row(bb), 0, 0)),
            pl.BlockSpec((1, d), lambda bb, i, j: (0, 0)),
            pl.BlockSpec((None, d, FFN_TF), lambda bb, i, j: (ffn_idx, 0, j)),
            pl.BlockSpec((None, d, FFN_TF), lambda bb, i, j: (ffn_idx, 0, j)),
            pl.BlockSpec((None, FFN_TF, d), lambda bb, i, j: (ffn_idx, j, 0)),
            pl.BlockSpec((1, d), lambda bb, i, j: (0, 0)),
        ],
        out_specs=pl.BlockSpec((1, tm, d), lambda bb, i, j: (bb, i, 0)),
        out_shape=jax.ShapeDtypeStruct((b, s, d), F32),
        scratch_shapes=[pltpu.VMEM((tm, d), BF16), pltpu.VMEM((tm, d), F32)],
        compiler_params=_params("parallel", "parallel", "arbitrary"),
        name="ffn",
    )(x, mod, norm_g.reshape(1, d), wg, wu, wd, final_g.reshape(1, d))


PROJ_TN = 1024
PROJ_TILE_KV = OFF_K // PROJ_TN


def _proj_kernel(x_ref, mod_ref, g_ref, w_ref, qkg_ref, cos_ref, sin_ref, o_ref, h_scr,
                 *, mod_base, col0, rope):
    jj = pl.program_id(2)
    j = jj + col0

    @pl.when(jj == 0)
    def _():
        _norm_modulate(x_ref, g_ref, mod_ref, mod_base, h_scr)

    z = _dot(h_scr[...], w_ref[...])

    def head_norm(t, gain):
        y = _rms(t) * gain
        if rope:
            y = y * cos_ref[...] + pltpu.roll(y, HEAD_DIM // 2, axis=1) * sin_ref[...]
        return y.astype(BF16)

    def store_heads(n_heads, gain):
        for hh in range(n_heads):
            sl = slice(hh * HEAD_DIM, (hh + 1) * HEAD_DIM)
            o_ref[0, :, sl] = head_norm(z[:, sl], gain)

    @pl.when(j < PROJ_TILE_KV)
    def _():
        store_heads(PROJ_TN // HEAD_DIM, qkg_ref[0:1, :])

    @pl.when(j == PROJ_TILE_KV)
    def _():
        store_heads(N_KV_HEADS, qkg_ref[1:2, :])
        o_ref[0, :, KV_W:] = z[:, KV_W:].astype(BF16)

    @pl.when(j > PROJ_TILE_KV)
    def _():
        o_ref[0] = z.astype(BF16)


def _proj(x, mod, mod_row, norm_g, w_in, qkg, cosf, sinf, *, col0, ncol, rope, tm):
    b, s, d = x.shape
    kern = functools.partial(_proj_kernel, mod_base=3, col0=col0, rope=rope)
    return pl.pallas_call(
        kern,
        grid=(b, s // tm, ncol),
        in_specs=[
            pl.BlockSpec((1, tm, d), lambda bb, i, j: (bb, i, 0)),
            pl.BlockSpec((1, N_MOD, d), lambda bb, i, j: (mod_row(bb), 0, 0)),
            pl.BlockSpec((1, d), lambda bb, i, j: (0, 0)),
            pl.BlockSpec((d, PROJ_TN), lambda bb, i, j: (0, j + col0)),
            pl.BlockSpec((2, HEAD_DIM), lambda bb, i, j: (0, 0)),
            pl.BlockSpec((tm, HEAD_DIM), lambda bb, i, j: (i, 0)),
            pl.BlockSpec((tm, HEAD_DIM), lambda bb, i, j: (i, 0)),
        ],
        out_specs=pl.BlockSpec((1, tm, PROJ_TN), lambda bb, i, j: (bb, i, j)),
        out_shape=jax.ShapeDtypeStruct((b, s, ncol * PROJ_TN), BF16),
        scratch_shapes=[pltpu.VMEM((tm, d), BF16)],
        compiler_params=_params("parallel", "parallel", "arbitrary"),
        name="proj",
    )(x, mod, norm_g.reshape(1, d), w_in, qkg, cosf, sinf)


ATTN_TQ = 256


def _attn_kernel(q_ref, qn_ref, k_ref, v_ref, kc_ref, vc_ref, o_ref,
                 vt_scr, vct_scr, s_scr, sc_scr, m_scr):
    i = pl.program_id(2)
    tq = q_ref.shape[1]
    k = k_ref[0]
    kc = kc_ref[0]
    nt = (((1,), (1,)), ((), ()))
    pair_w = 2 * HEAD_DIM

    def stack_pair(ref, col0):
        return jnp.concatenate([ref[0, :, col0:col0 + HEAD_DIM],
                                ref[0, :, col0 + HEAD_DIM:col0 + pair_w]], axis=0)

    def scores(q2):
        s1 = lax.dot_general(k, q2, nt, preferred_element_type=F32)
        s2 = lax.dot_general(kc, q2, nt, preferred_element_type=F32)
        m = jnp.maximum(jnp.max(s1, axis=0, keepdims=True), jnp.max(s2, axis=0, keepdims=True))
        return s1, s2, m

    def finish(s1, s2, m, col0):
        p1 = jnp.exp2(s1 - m)
        p2 = jnp.exp2(s2 - m)
        l = jnp.sum(p1, axis=0, keepdims=True) + jnp.sum(p2, axis=0, keepdims=True)
        ot = _dot(vt_scr[...], p1.astype(BF16)) + _dot(vct_scr[...], p2.astype(BF16))
        ot = ot * (1.0 / l)
        for e in range(2):
            o_ref[0, :, col0 + e * HEAD_DIM:col0 + (e + 1) * HEAD_DIM] = (
                ot[:, e * tq:(e + 1) * tq].T.astype(BF16))

    @pl.when(i == 0)
    def _():
        vt_scr[...] = v_ref[0].T
        vct_scr[...] = vc_ref[0].T
        s_scr[...], sc_scr[...], m_scr[...] = scores(stack_pair(q_ref, 0))

    s1b, s2b, mb = scores(stack_pair(q_ref, pair_w))
    finish(s_scr[...], sc_scr[...], m_scr[...], 0)
    s1n, s2n, mn = scores(stack_pair(qn_ref, 0))
    finish(s1b, s2b, mb, pair_w)
    s_scr[...] = s1n
    sc_scr[...] = s2n
    m_scr[...] = mn


def _attention(p, pc):
    b, s, _ = p.shape
    c = pc.shape[1]
    gw = GQA_GROUP * HEAD_DIM
    kb = OFF_K // HEAD_DIM
    vb = OFF_V // HEAD_DIM
    n = s // ATTN_TQ
    return pl.pallas_call(
        _attn_kernel,
        grid=(b, N_KV_HEADS, n),
        in_specs=[
            pl.BlockSpec((1, ATTN_TQ, gw), lambda bb, g, i: (bb, i, g)),
            pl.BlockSpec((1, ATTN_TQ, 2 * HEAD_DIM),
                         lambda bb, g, i: (bb, jnp.minimum(i + 1, n - 1), 2 * g)),
            pl.BlockSpec((1, s, HEAD_DIM), lambda bb, g, i: (bb, 0, kb + g)),
            pl.BlockSpec((1, s, HEAD_DIM), lambda bb, g, i: (bb, 0, vb + g)),
            pl.BlockSpec((1, c, HEAD_DIM), lambda bb, g, i: (bb, 0, g)),
            pl.BlockSpec((1, c, HEAD_DIM), lambda bb, g, i: (bb, 0, N_KV_HEADS + g)),
        ],
        out_specs=pl.BlockSpec((1, ATTN_TQ, gw), lambda bb, g, i: (bb, i, g)),
        out_shape=jax.ShapeDtypeStruct((b, s, ATTN_W), BF16),
        scratch_shapes=[
            pltpu.VMEM((HEAD_DIM, s), BF16),
            pltpu.VMEM((HEAD_DIM, c), BF16),
            pltpu.VMEM((s, 2 * ATTN_TQ), F32),
            pltpu.VMEM((c, 2 * ATTN_TQ), F32),
            pltpu.VMEM((1, 2 * ATTN_TQ), F32),
        ],
        compiler_params=_params("arbitrary", "arbitrary", "arbitrary"),
        name="attention",
    )(p, p, p, p, pc, pc)


LRU_CB = 512
LRU_TL = 256
LRU_GROUP = 2
HALO = SUBLANES
CONV_LEFT = 2


def _gelu_tanh(x):
    return 0.5 * x * (1.0 + jnp.tanh(0.7978845608028654 * (x + 0.044715 * x * x * x)))


def _lru_kernel(*refs, reverse, gated, nb):
    if gated:
        (x_ref, xp_ref, xn_ref, cw_ref, cb_ref, w_ref, ba_ref, bx_ref, lam_ref, h0_ref,
         hf_ref, lg_ref, o_ref, hl_ref, a_scr, u_scr, ho_scr, h_scr) = refs
    else:
        (x_ref, xp_ref, xn_ref, cw_ref, cb_ref, w_ref, ba_ref, bx_ref, lam_ref, h0_ref,
         o_ref, hl_ref, a_scr, u_scr, ho_scr, h_scr) = refs
    i = pl.program_id(1)
    n = pl.num_programs(1)
    ci = n - 1 - i if reverse else i
    tl = x_ref.shape[1]
    cb = x_ref.shape[2]

    def slot_rows(blk, bb):
        return pl.ds((blk % LRU_GROUP) * nb + bb, tl, stride=SUBLANES)

    @pl.when(i == 0)
    def _():
        for blk in range(cb // LRU_BLOCK_DIM):
            e = blk % LRU_GROUP
            h_scr[blk // LRU_GROUP, e * nb:(e + 1) * nb, :] = (
                h0_ref[:, blk * LRU_BLOCK_DIM:(blk + 1) * LRU_BLOCK_DIM])

    ri = lax.broadcasted_iota(jnp.int32, (tl, tl), 0)
    cj = lax.broadcasted_iota(jnp.int32, (tl, tl), 1)
    shifts = jnp.concatenate(
        [(cj == ri + d).astype(BF16) for d in (-2, -1, 1)], axis=0)
    row8 = lax.broadcasted_iota(jnp.int32, (HALO, cb), 0)
    has_prev = (ci > 0).astype(F32)
    has_next = (ci < n - 1).astype(F32)
    w0, w1, w2, w3 = (cw_ref[k:k + 1, :] for k in range(4))
    xcs = []
    for bb in range(nb):
        xb = x_ref[bb]
        sh = _dot(shifts, xb)
        y = (cb_ref[...] + w2 * xb.astype(F32) + w0 * sh[:tl] + w1 * sh[tl:2 * tl]
             + w3 * sh[2 * tl:])
        prev = xp_ref[bb].astype(F32) * has_prev
        nxt = xn_ref[bb].astype(F32) * has_next
        head = (w0 * jnp.where(row8 < 2, pltpu.roll(prev, 2, axis=0), 0.0)
                + w1 * jnp.where(row8 < 1, pltpu.roll(prev, 1, axis=0), 0.0))
        tail = w3 * jnp.where(row8 == HALO - 1, pltpu.roll(nxt, HALO - 1, axis=0), 0.0)
        xcs.append(jnp.concatenate(
            [y[:HALO] + head, y[HALO:tl - HALO], y[tl - HALO:] + tail], axis=0))
    xc = jnp.concatenate(xcs, axis=0)

    lam = lam_ref[...]
    neg = -lam
    sp = jnp.maximum(neg, 0.0) + jnp.log1p(jnp.exp(-jnp.abs(neg)))
    half_c = (-0.5 * LRU_C * LOG2_E) * sp
    for blk in range(cb // LRU_BLOCK_DIM):
        sl = slice(blk * LRU_BLOCK_DIM, (blk + 1) * LRU_BLOCK_DIM)
        xb = xc[:, sl]
        z = _dot(xb.astype(BF16), w_ref[blk])
        tr = jnp.tanh(z[:, :LRU_BLOCK_DIM] + ba_ref[:, sl])
        ti = jnp.tanh(z[:, LRU_BLOCK_DIM:] + bx_ref[:, sl])
        hc = half_c[:, sl]
        a = jnp.exp2(hc + hc * tr)
        hx = 0.5 * xb
        y = 1.0 - a * a
        root = jnp.where(y > 0.0, y * lax.rsqrt(y), 0.0)
        u = root * (hx + hx * ti)
        for bb in range(nb):
            a_scr[blk // LRU_GROUP, slot_rows(blk, bb), :] = a[bb * tl:(bb + 1) * tl]
            u_scr[blk // LRU_GROUP, slot_rows(blk, bb), :] = u[bb * tl:(bb + 1) * tl]

    ngrp = cb // LRU_BLOCK_DIM // LRU_GROUP

    def step(t, hs):
        t1 = tl - 1 - 2 * t if reverse else 2 * t
        t2 = t1 - 1 if reverse else t1 + 1
        rows1 = pl.ds(pl.multiple_of(t1 * SUBLANES, SUBLANES), SUBLANES)
        rows2 = pl.ds(pl.multiple_of(t2 * SUBLANES, SUBLANES), SUBLANES)
        out = []
        for g in range(ngrp):
            a1 = a_scr[g, rows1, :]
            u1 = u_scr[g, rows1, :]
            a2 = a_scr[g, rows2, :]
            u2 = u_scr[g, rows2, :]
            h2 = (a2 * a1) * hs[g] + (a2 * u1 + u2)
            ho_scr[g, rows1, :] = a1 * hs[g] + u1
            ho_scr[g, rows2, :] = h2
            out.append(h2)
        return tuple(out)

    hs = lax.fori_loop(0, tl // 2, step, tuple(h_scr[g] for g in range(ngrp)), unroll=4)
    for g in range(ngrp):
        h_scr[g] = hs[g]
    for blk in range(cb // LRU_BLOCK_DIM):
        e = blk % LRU_GROUP
        hl_ref[:, blk * LRU_BLOCK_DIM:(blk + 1) * LRU_BLOCK_DIM] = (
            hs[blk // LRU_GROUP][e * nb:(e + 1) * nb])

    for bb in range(nb):
        hb = jnp.concatenate(
            [ho_scr[blk // LRU_GROUP, slot_rows(blk, bb), :]
             for blk in range(cb // LRU_BLOCK_DIM)], axis=-1)
        if gated:
            hb = (hb + hf_ref[bb].astype(F32)) * _gelu_tanh(lg_ref[bb].astype(F32))
        o_ref[bb] = hb.astype(BF16)


def _lru(p, x_col0, conv_w, conv_b, w_blk, ba, bx, lam, h0, hf=None, lg_col0=None, *, reverse):
    nb, t, _ = p.shape
    tl = min(LRU_TL, t)
    n = t // tl
    cb = LRU_CB
    xb0 = x_col0 // cb
    gated = hf is not None
    hpc = tl // HALO
    assert nb * LRU_GROUP == SUBLANES
    ngrp = cb // LRU_BLOCK_DIM // LRU_GROUP

    def pos(i):
        return n - 1 - i if reverse else i

    in_specs = [
        pl.BlockSpec((nb, tl, cb), lambda c, i: (0, pos(i), xb0 + c)),
        pl.BlockSpec((nb, HALO, cb), lambda c, i: (0, jnp.maximum(pos(i) * hpc - 1, 0), xb0 + c)),
        pl.BlockSpec((nb, HALO, cb),
                     lambda c, i: (0, jnp.minimum((pos(i) + 1) * hpc, t // HALO - 1), xb0 + c)),
        pl.BlockSpec((4, cb), lambda c, i: (0, c)),
        pl.BlockSpec((1, cb), lambda c, i: (0, c)),
        pl.BlockSpec((cb // LRU_BLOCK_DIM, LRU_BLOCK_DIM, 2 * LRU_BLOCK_DIM), lambda c, i: (c, 0, 0)),
        pl.BlockSpec((1, cb), lambda c, i: (0, c)),
        pl.BlockSpec((1, cb), lambda c, i: (0, c)),
        pl.BlockSpec((1, cb), lambda c, i: (0, c)),
        pl.BlockSpec((nb, cb), lambda c, i: (0, c)),
    ]
    args = [p, p, p, conv_w, conv_b.reshape(1, -1), w_blk, ba.reshape(1, -1), bx.reshape(1, -1),
            lam.reshape(1, -1), h0]
    if gated:
        lb0 = lg_col0 // cb
        in_specs += [
            pl.BlockSpec((nb, tl, cb), lambda c, i: (0, pos(i), c)),
            pl.BlockSpec((nb, tl, cb), lambda c, i: (0, pos(i), lb0 + c)),
        ]
        args += [hf, p]
    kern = functools.partial(_lru_kernel, reverse=reverse, gated=gated, nb=nb)
    return pl.pallas_call(
        kern,
        grid=(LRU_W // cb, n),
        in_specs=in_specs,
        out_specs=[
            pl.BlockSpec((nb, tl, cb), lambda c, i: (0, pos(i), c)),
            pl.BlockSpec((nb, cb), lambda c, i: (0, c)),
        ],
        out_shape=[
            jax.ShapeDtypeStruct((nb, t, LRU_W), BF16),
            jax.ShapeDtypeStruct((nb, LRU_W), F32),
        ],
        scratch_shapes=[
            pltpu.VMEM((ngrp, tl * SUBLANES, LRU_BLOCK_DIM), F32),
            pltpu.VMEM((ngrp, tl * SUBLANES, LRU_BLOCK_DIM), F32),
            pltpu.VMEM((ngrp, tl * SUBLANES, LRU_BLOCK_DIM), F32),
            pltpu.VMEM((ngrp, SUBLANES, LRU_BLOCK_DIM), F32),
        ],
        compiler_params=_params("parallel", "arbitrary"),
        name="lru_bwd" if reverse else "lru_fwd",
    )(*args)


MERGE_TM = 512
MERGE_TK = 1024


def _merge_kernel(x_ref, mod_ref, attn_ref, lru_ref, ga_ref, gl_ref, w_ref, o_ref, acc_scr):
    kk = pl.program_id(2)

    @pl.when(kk == 0)
    def _():
        acc_scr[...] = jnp.zeros_like(acc_scr)

    m = (attn_ref[0] * (1.0 + jnp.tanh(ga_ref[0])) + lru_ref[0] * (1.0 + jnp.tanh(gl_ref[0])))
    acc_scr[...] += _dot(m, w_ref[...])

    @pl.when(kk == pl.num_programs(2) - 1)
    def _():
        gate = mod_ref[0, 5:6, :]

        def body(rows):
            o_ref[0, rows, :] = x_ref[0, rows, :] + gate * acc_scr[rows, :]

        _for_row_chunks(acc_scr.shape[0], body)


def _merge(x, mod, attn, lru, p, w_out):
    b, s, d = x.shape
    ga0 = OFF_GA // MERGE_TK
    gl0 = OFF_GL // MERGE_TK
    return pl.pallas_call(
        _merge_kernel,
        grid=(b, s // MERGE_TM, d // MERGE_TK),
        in_specs=[
            pl.BlockSpec((1, MERGE_TM, d), lambda bb, i, k: (bb, i, 0)),
            pl.BlockSpec((1, N_MOD, d), lambda bb, i, k: (bb, 0, 0)),
            pl.BlockSpec((1, MERGE_TM, MERGE_TK), lambda bb, i, k: (bb, i, k)),
            pl.BlockSpec((1, MERGE_TM, MERGE_TK), lambda bb, i, k: (bb, i, k)),
            pl.BlockSpec((1, MERGE_TM, MERGE_TK), lambda bb, i, k: (bb, i, ga0 + k)),
            pl.BlockSpec((1, MERGE_TM, MERGE_TK), lambda bb, i, k: (bb, i, gl0 + k)),
            pl.BlockSpec((MERGE_TK, d), lambda bb, i, k: (k, 0)),
        ],
        out_specs=pl.BlockSpec((1, MERGE_TM, d), lambda bb, i, k: (bb, i, 0)),
        out_shape=jax.ShapeDtypeStruct((b, s, d), F32),
        scratch_shapes=[pltpu.VMEM((MERGE_TM, d), F32)],
        compiler_params=_params("parallel", "parallel", "arbitrary"),
        name="merge",
    )(x, mod, attn, lru, p, p, w_out)


def _deinterleave_heads(w, n_heads):
    lead = w.shape[:-1]
    w = w.reshape(lead + (n_heads, HEAD_DIM // 2, 2))
    w = jnp.swapaxes(w, -1, -2)
    return w.reshape(lead + (n_heads * HEAD_DIM,))


def _rope_tables(n_tok):
    rows = n_tok // GRID_W
    row = jnp.repeat(jnp.arange(rows, dtype=F32), GRID_W)
    col = jnp.tile(jnp.arange(GRID_W, dtype=F32), rows)
    axis_dims = HEAD_DIM // 2
    freqs = ROPE_THETA ** (-jnp.arange(0, axis_dims, 2, dtype=F32) / axis_dims)
    ang = jnp.concatenate([row[:, None] * freqs, col[:, None] * freqs], axis=-1)
    cos, sin = jnp.cos(ang), jnp.sin(ang)
    return jnp.concatenate([cos, cos], axis=-1), jnp.concatenate([-sin, sin], axis=-1)


def kernel(x, c, ctx, c_ctx, w_mod, b_mod, norm_g, ffn_wg, ffn_wu, ffn_wd, w_in, w_out, q_norm_g,
           k_norm_g, conv_w, conv_b, lru_wa, lru_ba, lru_wx, lru_bx, lru_lambda, final_norm_g):
    b, s, d = x.shape
    n_ctx = ctx.shape[1]
    ctx_row = b

    cc = jnp.concatenate([c, c_ctx[None], jnp.zeros((SUBLANES - b - 1, d), F32)], axis=0)
    mod = _modulation(cc, w_mod[0], b_mod[0]).reshape(SUBLANES, N_MOD, d)

    wg = ffn_wg[0].astype(BF16)
    wu = ffn_wu[0].astype(BF16)
    wd = ffn_wd[0].astype(BF16)
    w_in_p = jnp.concatenate([
        _deinterleave_heads(w_in[0][:, OFF_Q:OFF_K], N_Q_HEADS),
        _deinterleave_heads(w_in[0][:, OFF_K:OFF_V], N_KV_HEADS),
        w_in[0][:, OFF_V:OFF_GA],
        0.5 * w_in[0][:, OFF_GA:],
    ], axis=1).astype(BF16)
    w_out_b = (0.5 * w_out[0]).astype(BF16)
    qkg = jnp.stack([
        _deinterleave_heads(q_norm_g[0], 1) * (HEAD_DIM ** -0.5 * LOG2_E),
        _deinterleave_heads(k_norm_g[0], 1),
    ])
    cosf, sinf = _rope_tables(s)
    w_lru = (0.5 * jnp.concatenate([lru_wa[0], lru_wx[0]], axis=-1)).astype(BF16)
    ba_h = 0.5 * lru_ba[0]
    bx_h = 0.5 * lru_bx[0]

    lat_row = lambda bb: bb
    ctx_rowf = lambda bb: ctx_row

    x1 = _ffn(x, mod, lat_row, norm_g[0, 0], wg, wu, wd, final_norm_g,
              ffn_idx=0, mod_base=0, final_norm=False, tm=512)
    ctx_flat = ctx.reshape(1, b * n_ctx, d)
    ctx1 = _ffn(ctx_flat, mod, ctx_rowf, norm_g[0, 0], wg, wu, wd, final_norm_g,
                ffn_idx=0, mod_base=0, final_norm=False, tm=512)

    pc = _proj(ctx1, mod, ctx_rowf, norm_g[0, 1], w_in_p, qkg, cosf, sinf,
               col0=PROJ_TILE_KV, ncol=3, rope=False, tm=b * n_ctx)
    pc = pc.reshape(b, n_ctx, 3 * PROJ_TN)
    zeros = jnp.zeros((b, LRU_W), F32)
    lx_c = OFF_LX - OFF_K
    _, hf0 = _lru(pc, lx_c, conv_w[0], conv_b[0], w_lru[0], ba_h[0], bx_h[0],
                  lru_lambda[0, 0], zeros, reverse=False)
    _, hb0 = _lru(pc, lx_c, conv_w[0], conv_b[0], w_lru[1], ba_h[1], bx_h[1],
                  lru_lambda[0, 1], zeros, reverse=True)

    p = _proj(x1, mod, lat_row, norm_g[0, 1], w_in_p, qkg, cosf, sinf,
              col0=0, ncol=IN_W // PROJ_TN, rope=True, tm=1024)
    attn = _attention(p, pc)
    hf, _ = _lru(p, OFF_LX, conv_w[0], conv_b[0], w_lru[0], ba_h[0], bx_h[0],
                 lru_lambda[0, 0], hf0, reverse=False)
    lru, _ = _lru(p, OFF_LX, conv_w[0], conv_b[0], w_lru[1], ba_h[1], bx_h[1],
                  lru_lambda[0, 1], hb0, hf=hf, lg_col0=OFF_LG, reverse=True)
    x2 = _merge(x1, mod, attn, lru, p, w_out_b)

    return _ffn(x2, mod, lat_row, norm_g[0, 2], wg, wu, wd, final_norm_g,
                ffn_idx=1, mod_base=6, final_norm=True, tm=512)
```

```python
import functools

import jax
import jax.numpy as jnp
from jax import lax
from jax.experimental import pallas as pl
from jax.experimental.pallas import tpu as pltpu

D_MODEL = 2048
GRID_W = 64
HEAD_DIM = 128
N_Q_HEADS = 16
N_KV_HEADS = 4
GQA_GROUP = N_Q_HEADS // N_KV_HEADS
ATTN_W = N_Q_HEADS * HEAD_DIM
KV_W = N_KV_HEADS * HEAD_DIM
LRU_W = D_MODEL
LRU_BLOCK_DIM = 128
LRU_C = 8.0
D_FF = 5632
ROPE_THETA = 10000.0
EPS = 1e-6
N_MOD = 9
FFN_RES = 0.5
LOG2_E = 1.4426950408889634
OFF_Q = 0
OFF_K = OFF_Q + ATTN_W
OFF_V = OFF_K + KV_W
OFF_LX = OFF_V + KV_W
OFF_LG = OFF_LX + LRU_W
OFF_GA = OFF_LG + LRU_W
OFF_GL = OFF_GA + D_MODEL
IN_W = OFF_GL + D_MODEL

SUBLANES = 8
LANES = 128
VMEM_LIMIT = 56 * 1024 * 1024

BF16 = jnp.bfloat16
F32 = jnp.float32


def _params(*sem):
    return pltpu.CompilerParams(dimension_semantics=sem, vmem_limit_bytes=VMEM_LIMIT)


def _dot(a, b):
    return jnp.dot(a, b, preferred_element_type=F32)


def _rms(x):
    return x * lax.rsqrt(jnp.mean(x * x, axis=-1, keepdims=True) + EPS)


ROW_CHUNK = 16


def _for_row_chunks(n_rows, body):
    def step(r, carry):
        body(pl.ds(pl.multiple_of(r * ROW_CHUNK, ROW_CHUNK), ROW_CHUNK))
        return carry
    lax.fori_loop(0, n_rows // ROW_CHUNK, step, 0, unroll=8)


def _norm_modulate(x_ref, g_ref, mod_ref, mod_base, h_scr):
    shift = mod_ref[0, mod_base:mod_base + 1, :]
    gain = g_ref[...] * (1.0 + mod_ref[0, mod_base + 1:mod_base + 2, :])

    def body(rows):
        x = x_ref[0, rows, :]
        rs = lax.rsqrt(jnp.mean(x * x, axis=-1, keepdims=True) + EPS)
        h_scr[rows, :] = (x * rs * gain + shift).astype(BF16)

    _for_row_chunks(h_scr.shape[0], body)


MOD_TN = 1024


def _mod_kernel(c_ref, w_ref, b_ref, o_ref):
    c = c_ref[...]
    s = c * jax.nn.sigmoid(c)
    o_ref[...] = jnp.dot(s, w_ref[...], preferred_element_type=F32,
                         precision=lax.Precision.HIGHEST) + b_ref[...]


def _modulation(cc, w_mod, b_mod):
    rows, d = cc.shape
    n = w_mod.shape[1]
    return pl.pallas_call(
        _mod_kernel,
        grid=(n // MOD_TN,),
        in_specs=[
            pl.BlockSpec((rows, d), lambda j: (0, 0)),
            pl.BlockSpec((d, MOD_TN), lambda j: (0, j)),
            pl.BlockSpec((1, MOD_TN), lambda j: (0, j)),
        ],
        out_specs=pl.BlockSpec((rows, MOD_TN), lambda j: (0, j)),
        out_shape=jax.ShapeDtypeStruct((rows, n), F32),
        compiler_params=_params("arbitrary"),
        name="modulation",
    )(cc, w_mod, b_mod.reshape(1, n))


FFN_TF = 512


def _ffn_kernel(x_ref, mod_ref, g_ref, wg_ref, wu_ref, wd_ref, fg_ref, o_ref, h_scr, acc_scr,
                *, mod_base, final_norm):
    j = pl.program_id(2)

    @pl.when(j == 0)
    def _():
        _norm_modulate(x_ref, g_ref, mod_ref, mod_base, h_scr)
        acc_scr[...] = jnp.zeros_like(acc_scr)

    h = h_scr[...]
    gate = _dot(h, wg_ref[...])
    up = _dot(h, wu_ref[...])
    act = (gate * jax.nn.sigmoid(gate) * up).astype(BF16)
    acc_scr[...] += _dot(act, wd_ref[...])

    @pl.when(j == pl.num_programs(2) - 1)
    def _():
        res_gate = FFN_RES * mod_ref[0, mod_base + 2:mod_base + 3, :]

        def body(rows):
            xo = x_ref[0, rows, :] + res_gate * acc_scr[rows, :]
            if final_norm:
                xo = _rms(xo) * fg_ref[...]
            o_ref[0, rows, :] = xo

        _for_row_chunks(acc_scr.shape[0], body)


def _ffn(x, mod, mod_row, norm_g, wg, wu, wd, final_g, *, ffn_idx, mod_base, final_norm, tm):
    b, s, d = x.shape
    kern = functools.partial(_ffn_kernel, mod_base=mod_base, final_norm=final_norm)
    return pl.pallas_call(
        kern,
        grid=(b, s // tm, D_FF // FFN_TF),
        in_specs=[
            pl.BlockSpec((1, tm, d), lambda bb, i, j: (bb, i, 0)),
            pl.BlockSpec((1, N_MOD, d), lambda bb, i, j: (mod_row(bb), 0, 0)),
            pl.BlockSpec((1, d), lambda bb, i, j: (0, 0)),
            pl.BlockSpec((None, d, FFN_TF), lambda bb, i, j: (ffn_idx, 0, j)),
            pl.BlockSpec((None, d, FFN_TF), lambda bb, i, j: (ffn_idx, 0, j)),
            pl.BlockSpec((None, FFN_TF, d), lambda bb, i, j: (ffn_idx, j, 0)),
            pl.BlockSpec((1, d), lambda bb, i, j: (0, 0)),
        ],
        out_specs=pl.BlockSpec((1, tm, d), lambda bb, i, j: (bb, i, 0)),
        out_shape=jax.ShapeDtypeStruct((b, s, d), F32),
        scratch_shapes=[pltpu.VMEM((tm, d), BF16), pltpu.VMEM((tm, d), F32)],
        compiler_params=_params("parallel", "parallel", "arbitrary"),
        name="ffn",
    )(x, mod, norm_g.reshape(1, d), wg, wu, wd, final_g.reshape(1, d))


PROJ_TN = 1024
PROJ_TILE_KV = OFF_K // PROJ_TN


def _proj_kernel(x_ref, mod_ref, g_ref, w_ref, qkg_ref, cos_ref, sin_ref, o_ref, h_scr,
                 *, mod_base, col0, rope):
    jj = pl.program_id(2)
    j = jj + col0

    @pl.when(jj == 0)
    def _():
        _norm_modulate(x_ref, g_ref, mod_ref, mod_base, h_scr)

    z = _dot(h_scr[...], w_ref[...])

    def head_norm(t, gain):
        y = _rms(t) * gain
        if rope:
            y = y * cos_ref[...] + pltpu.roll(y, HEAD_DIM // 2, axis=1) * sin_ref[...]
        return y.astype(BF16)

    def store_heads(n_heads, gain):
        for hh in range(n_heads):
            sl = slice(hh * HEAD_DIM, (hh + 1) * HEAD_DIM)
            o_ref[0, :, sl] = head_norm(z[:, sl], gain)

    @pl.when(j < PROJ_TILE_KV)
    def _():
        store_heads(PROJ_TN // HEAD_DIM, qkg_ref[0:1, :])

    @pl.when(j == PROJ_TILE_KV)
    def _():
        store_heads(N_KV_HEADS, qkg_ref[1:2, :])
        o_ref[0, :, KV_W:] = z[:, KV_W:].astype(BF16)

    @pl.when(j > PROJ_TILE_KV)
    def _():
        o_ref[0] = z.astype(BF16)


def _proj(x, mod, mod_row, norm_g, w_in, qkg, cosf, sinf, *, col0, ncol, rope, tm):
    b, s, d = x.shape
    kern = functools.partial(_proj_kernel, mod_base=3, col0=col0, rope=rope)
    return pl.pallas_call(
        kern,
        grid=(b, s // tm, ncol),
        in_specs=[
            pl.BlockSpec((1, tm, d), lambda bb, i, j: (bb, i, 0)),
            pl.BlockSpec((1, N_MOD, d), lambda bb, i, j: (mod_row(bb), 0, 0)),
            pl.BlockSpec((1, d), lambda bb, i, j: (0, 0)),
            pl.BlockSpec((d, PROJ_TN), lambda bb, i, j: (0, j + col0)),
            pl.BlockSpec((2, HEAD_DIM), lambda bb, i, j: (0, 0)),
            pl.BlockSpec((tm, HEAD_DIM), lambda bb, i, j: (i, 0)),
            pl.BlockSpec((tm, HEAD_DIM), lambda bb, i, j: (i, 0)),
        ],
        out_specs=pl.BlockSpec((1, tm, PROJ_TN), lambda bb, i, j: (bb, i, j)),
        out_shape=jax.ShapeDtypeStruct((b, s, ncol * PROJ_TN), BF16),
        scratch_shapes=[pltpu.VMEM((tm, d), BF16)],
        compiler_params=_params("parallel", "parallel", "arbitrary"),
        name="proj",
    )(x, mod, norm_g.reshape(1, d), w_in, qkg, cosf, sinf)


ATTN_TQ = 256
ATTN_KEY_CHUNK = 256


def _attn_kernel(q_ref, qn_ref, k_ref, v_ref, kc_ref, vc_ref, o_ref,
                 vt_scr, vct_scr, s_scr, m_scr):
    i = pl.program_id(2)
    tq = q_ref.shape[1]
    s_len = k_ref.shape[1]
    c_len = kc_ref.shape[1]
    nt = (((1,), (1,)), ((), ()))
    pair_w = 2 * HEAD_DIM
    chunks = [(k_ref, vt_scr, r, ATTN_KEY_CHUNK, r) for r in range(0, s_len, ATTN_KEY_CHUNK)]
    chunks.append((kc_ref, vct_scr, 0, c_len, s_len))

    def stack_pair(ref, col0):
        return jnp.concatenate([ref[0, :, col0:col0 + HEAD_DIM],
                                ref[0, :, col0 + HEAD_DIM:col0 + pair_w]], axis=0)

    def fold8(x, op):
        return op(x.reshape(x.shape[0] // SUBLANES, SUBLANES, x.shape[1]), axis=0)

    def chunk_scores(q2, slot, chunk, mpart):
        kref, _, r, n, row = chunk
        s = lax.dot_general(kref[0, r:r + n, :], q2, nt, preferred_element_type=F32)
        s_scr[slot, row:row + n, :] = s
        cm = fold8(s, jnp.max)
        return cm if mpart is None else jnp.maximum(mpart, cm)

    def phase(q_next, slot_next, slot_cur, col0):
        m_cur = m_scr[slot_cur]
        mpart = lpart = ot = None
        for chunk in chunks:
            _, vtref, r, n, row = chunk
            mpart = chunk_scores(q_next, slot_next, chunk, mpart)
            p = jnp.exp2(s_scr[slot_cur, row:row + n, :] - m_cur)
            cl = fold8(p, jnp.sum)
            lpart = cl if lpart is None else lpart + cl
            pv = _dot(vtref[:, r:r + n], p.astype(BF16))
            ot = pv if ot is None else ot + pv
        m_scr[slot_next] = jnp.max(mpart, axis=0, keepdims=True)
        ot = ot * (1.0 / jnp.sum(lpart, axis=0, keepdims=True))
        for e in range(2):
            o_ref[0, :, col0 + e * HEAD_DIM:col0 + (e + 1) * HEAD_DIM] = (
                ot[:, e * tq:(e + 1) * tq].T.astype(BF16))

    @pl.when(i == 0)
    def _():
        vt_scr[...] = v_ref[0].T
        vct_scr[...] = vc_ref[0].T
        q2 = stack_pair(q_ref, 0)
        mpart = None
        for chunk in chunks:
            mpart = chunk_scores(q2, 0, chunk, mpart)
        m_scr[0] = jnp.max(mpart, axis=0, keepdims=True)

    phase(stack_pair(q_ref, pair_w), 1, 0, 0)
    phase(stack_pair(qn_ref, 0), 0, 1, pair_w)


def _attention(p, pc):
    b, s, _ = p.shape
    c = pc.shape[1]
    gw = GQA_GROUP * HEAD_DIM
    kb = OFF_K // HEAD_DIM
    vb = OFF_V // HEAD_DIM
    n = s // ATTN_TQ
    return pl.pallas_call(
        _attn_kernel,
        grid=(b, N_KV_HEADS, n),
        in_specs=[
            pl.BlockSpec((1, ATTN_TQ, gw), lambda bb, g, i: (bb, i, g)),
            pl.BlockSpec((1, ATTN_TQ, 2 * HEAD_DIM),
                         lambda bb, g, i: (bb, jnp.minimum(i + 1, n - 1), 2 * g)),
            pl.BlockSpec((1, s, HEAD_DIM), lambda bb, g, i: (bb, 0, kb + g)),
            pl.BlockSpec((1, s, HEAD_DIM), lambda bb, g, i: (bb, 0, vb + g)),
            pl.BlockSpec((1, c, HEAD_DIM), lambda bb, g, i: (bb, 0, g)),
            pl.BlockSpec((1, c, HEAD_DIM), lambda bb, g, i: (bb, 0, N_KV_HEADS + g)),
        ],
        out_specs=pl.BlockSpec((1, ATTN_TQ, gw), lambda bb, g, i: (bb, i, g)),
        out_shape=jax.ShapeDtypeStruct((b, s, ATTN_W), BF16),
        scratch_shapes=[
            pltpu.VMEM((HEAD_DIM, s), BF16),
            pltpu.VMEM((HEAD_DIM, c), BF16),
            pltpu.VMEM((2, s + c, 2 * ATTN_TQ), F32),
            pltpu.VMEM((2, 1, 2 * ATTN_TQ), F32),
        ],
        compiler_params=_params("arbitrary", "arbitrary", "arbitrary"),
        name="attention",
    )(p, p, p, p, pc, pc)


LRU_CB = 512
LRU_TL = 256
LRU_TINY = 1e-30
LRU_GROUP = 2
HALO = SUBLANES
CONV_LEFT = 2


def _gelu_tanh(x):
    return 0.5 * x * (1.0 + jnp.tanh(0.7978845608028654 * (x + 0.044715 * x * x * x)))


def _lru_kernel(*refs, reverse, gated, nb):
    if gated:
        (x_ref, xp_ref, xn_ref, cw_ref, cb_ref, w_ref, ba_ref, bx_ref, lam_ref, h0_ref,
         hf_ref, lg_ref, o_ref, hl_ref, a_scr, u_scr, ho_scr, h_scr) = refs
    else:
        (x_ref, xp_ref, xn_ref, cw_ref, cb_ref, w_ref, ba_ref, bx_ref, lam_ref, h0_ref,
         o_ref, hl_ref, a_scr, u_scr, ho_scr, h_scr) = refs
    i = pl.program_id(1)
    n = pl.num_programs(1)
    ci = n - 1 - i if reverse else i
    tl = x_ref.shape[1]
    cb = x_ref.shape[2]

    def slot_rows(blk, bb):
        return pl.ds((blk % LRU_GROUP) * nb + bb, tl, stride=SUBLANES)

    @pl.when(i == 0)
    def _():
        for blk in range(cb // LRU_BLOCK_DIM):
            e = blk % LRU_GROUP
            h_scr[blk // LRU_GROUP, e * nb:(e + 1) * nb, :] = (
                h0_ref[:, blk * LRU_BLOCK_DIM:(blk + 1) * LRU_BLOCK_DIM])

    ri = lax.broadcasted_iota(jnp.int32, (tl, tl), 0)
    cj = lax.broadcasted_iota(jnp.int32, (tl, tl), 1)
    shifts = jnp.concatenate(
        [(cj == ri + d).astype(BF16) for d in (-2, -1, 1)], axis=0)
    row8 = lax.broadcasted_iota(jnp.int32, (HALO, cb), 0)
    has_prev = (ci > 0).astype(F32)
    has_next = (ci < n - 1).astype(F32)
    w0, w1, w2, w3 = (cw_ref[k:k + 1, :] for k in range(4))
    xcs = []
    for bb in range(nb):
        xb = x_ref[bb]
        sh = _dot(shifts, xb)
        y = (cb_ref[...] + w2 * xb.astype(F32) + w0 * sh[:tl] + w1 * sh[tl:2 * tl]
             + w3 * sh[2 * tl:])
        prev = xp_ref[bb].astype(F32) * has_prev
        nxt = xn_ref[bb].astype(F32) * has_next
        head = (w0 * jnp.where(row8 < 2, pltpu.roll(prev, 2, axis=0), 0.0)
                + w1 * jnp.where(row8 < 1, pltpu.roll(prev, 1, axis=0), 0.0))
        tail = w3 * jnp.where(row8 == HALO - 1, pltpu.roll(nxt, HALO - 1, axis=0), 0.0)
        xcs.append(jnp.concatenate(
            [y[:HALO] + head, y[HALO:tl - HALO], y[tl - HALO:] + tail], axis=0))
    xc = jnp.concatenate(xcs, axis=0)

    lam = lam_ref[...]
    neg = -lam
    sp = jnp.maximum(neg, 0.0) + jnp.log1p(jnp.exp(-jnp.abs(neg)))
    half_c = (-0.5 * LRU_C * LOG2_E) * sp
    for blk in range(cb // LRU_BLOCK_DIM):
        sl = slice(blk * LRU_BLOCK_DIM, (blk + 1) * LRU_BLOCK_DIM)
        xb = xc[:, sl]
        z = _dot(xb.astype(BF16), w_ref[blk])
        tr = jnp.tanh(z[:, :LRU_BLOCK_DIM] + ba_ref[:, sl])
        ti = jnp.tanh(z[:, LRU_BLOCK_DIM:] + bx_ref[:, sl])
        hc = half_c[:, sl]
        a = jnp.exp2(hc + hc * tr)
        hx = 0.5 * xb
        y = 1.0 - a * a
        root = y * lax.rsqrt(jnp.maximum(y, LRU_TINY))
        u = root * (hx + hx * ti)
        for bb in range(nb):
            a_scr[blk // LRU_GROUP, slot_rows(blk, bb), :] = a[bb * tl:(bb + 1) * tl]
            u_scr[blk // LRU_GROUP, slot_rows(blk, bb), :] = u[bb * tl:(bb + 1) * tl]

    ngrp = cb // LRU_BLOCK_DIM // LRU_GROUP

    def step(t, hs):
        t1 = tl - 1 - 2 * t if reverse else 2 * t
        t2 = t1 - 1 if reverse else t1 + 1
        rows1 = pl.ds(pl.multiple_of(t1 * SUBLANES, SUBLANES), SUBLANES)
        rows2 = pl.ds(pl.multiple_of(t2 * SUBLANES, SUBLANES), SUBLANES)
        out = []
        for g in range(ngrp):
            a1 = a_scr[g, rows1, :]
            u1 = u_scr[g, rows1, :]
            a2 = a_scr[g, rows2, :]
            u2 = u_scr[g, rows2, :]
            h2 = (a2 * a1) * hs[g] + (a2 * u1 + u2)
            ho_scr[g, rows1, :] = a1 * hs[g] + u1
            ho_scr[g, rows2, :] = h2
            out.append(h2)
        return tuple(out)

    hs = lax.fori_loop(0, tl // 2, step, tuple(h_scr[g] for g in range(ngrp)), unroll=4)
    for g in range(ngrp):
        h_scr[g] = hs[g]
    for blk in range(cb // LRU_BLOCK_DIM):
        e = blk % LRU_GROUP
        hl_ref[:, blk * LRU_BLOCK_DIM:(blk + 1) * LRU_BLOCK_DIM] = (
            hs[blk // LRU_GROUP][e * nb:(e + 1) * nb])

    for bb in range(nb):
        hb = jnp.concatenate(
            [ho_scr[blk // LRU_GROUP, slot_rows(blk, bb), :]
             for blk in range(cb // LRU_BLOCK_DIM)], axis=-1)
        if gated:
            hb = (hb + hf_ref[bb].astype(F32)) * _gelu_tanh(lg_ref[bb].astype(F32))
        o_ref[bb] = hb.astype(BF16)


def _lru(p, x_col0, conv_w, conv_b, w_blk, ba, bx, lam, h0, hf=None, lg_col0=None, *, reverse):
    nb, t, _ = p.shape
    tl = min(LRU_TL, t)
    n = t // tl
    cb = LRU_CB
    xb0 = x_col0 // cb
    gated = hf is not None
    hpc = tl // HALO
    assert nb * LRU_GROUP == SUBLANES
    ngrp = cb // LRU_BLOCK_DIM // LRU_GROUP

    def pos(i):
        return n - 1 - i if reverse else i

    in_specs = [
        pl.BlockSpec((nb, tl, cb), lambda c, i: (0, pos(i), xb0 + c)),
        pl.BlockSpec((nb, HALO, cb), lambda c, i: (0, jnp.maximum(pos(i) * hpc - 1, 0), xb0 + c)),
        pl.BlockSpec((nb, HALO, cb),
                     lambda c, i: (0, jnp.minimum((pos(i) + 1) * hpc, t // HALO - 1), xb0 + c)),
        pl.BlockSpec((4, cb), lambda c, i: (0, c)),
        pl.BlockSpec((1, cb), lambda c, i: (0, c)),
        pl.BlockSpec((cb // LRU_BLOCK_DIM, LRU_BLOCK_DIM, 2 * LRU_BLOCK_DIM), lambda c, i: (c, 0, 0)),
        pl.BlockSpec((1, cb), lambda c, i: (0, c)),
        pl.BlockSpec((1, cb), lambda c, i: (0, c)),
        pl.BlockSpec((1, cb), lambda c, i: (0, c)),
        pl.BlockSpec((nb, cb), lambda c, i: (0, c)),
    ]
    args = [p, p, p, conv_w, conv_b.reshape(1, -1), w_blk, ba.reshape(1, -1), bx.reshape(1, -1),
            lam.reshape(1, -1), h0]
    if gated:
        lb0 = lg_col0 // cb
        in_specs += [
            pl.BlockSpec((nb, tl, cb), lambda c, i: (0, pos(i), c)),
            pl.BlockSpec((nb, tl, cb), lambda c, i: (0, pos(i), lb0 + c)),
        ]
        args += [hf, p]
    kern = functools.partial(_lru_kernel, reverse=reverse, gated=gated, nb=nb)
    return pl.pallas_call(
        kern,
        grid=(LRU_W // cb, n),
        in_specs=in_specs,
        out_specs=[
            pl.BlockSpec((nb, tl, cb), lambda c, i: (0, pos(i), c)),
            pl.BlockSpec((nb, cb), lambda c, i: (0, c)),
        ],
        out_shape=[
            jax.ShapeDtypeStruct((nb, t, LRU_W), BF16),
            jax.ShapeDtypeStruct((nb, LRU_W), F32),
        ],
        scratch_shapes=[
            pltpu.VMEM((ngrp, tl * SUBLANES, LRU_BLOCK_DIM), F32),
            pltpu.VMEM((ngrp, tl * SUBLANES, LRU_BLOCK_DIM), F32),
            pltpu.VMEM((ngrp, tl * SUBLANES, LRU_BLOCK_DIM), F32),
            pltpu.VMEM((ngrp, SUBLANES, LRU_BLOCK_DIM), F32),
        ],
        compiler_params=_params("parallel", "arbitrary"),
        name="lru_bwd" if reverse else "lru_fwd",
    )(*args)


MERGE_TM = 512
MERGE_TK = 1024


def _merge_kernel(x_ref, mod_ref, attn_ref, lru_ref, ga_ref, gl_ref, w_ref, o_ref, acc_scr):
    kk = pl.program_id(2)

    @pl.when(kk == 0)
    def _():
        acc_scr[...] = jnp.zeros_like(acc_scr)

    m = (attn_ref[0] * (1.0 + jnp.tanh(ga_ref[0])) + lru_ref[0] * (1.0 + jnp.tanh(gl_ref[0])))
    w_rows = pl.ds(pl.multiple_of(kk * MERGE_TK, MERGE_TK), MERGE_TK)
    acc_scr[...] += _dot(m, w_ref[w_rows, :])

    @pl.when(kk == pl.num_programs(2) - 1)
    def _():
        gate = mod_ref[0, 5:6, :]

        def body(rows):
            o_ref[0, rows, :] = x_ref[0, rows, :] + gate * acc_scr[rows, :]

        _for_row_chunks(acc_scr.shape[0], body)


def _merge(x, mod, attn, lru, p, w_out):
    b, s, d = x.shape
    ga0 = OFF_GA // MERGE_TK
    gl0 = OFF_GL // MERGE_TK
    return pl.pallas_call(
        _merge_kernel,
        grid=(b, s // MERGE_TM, d // MERGE_TK),
        in_specs=[
            pl.BlockSpec((1, MERGE_TM, d), lambda bb, i, k: (bb, i, 0)),
            pl.BlockSpec((1, N_MOD, d), lambda bb, i, k: (bb, 0, 0)),
            pl.BlockSpec((1, MERGE_TM, MERGE_TK), lambda bb, i, k: (bb, i, k)),
            pl.BlockSpec((1, MERGE_TM, MERGE_TK), lambda bb, i, k: (bb, i, k)),
            pl.BlockSpec((1, MERGE_TM, MERGE_TK), lambda bb, i, k: (bb, i, ga0 + k)),
            pl.BlockSpec((1, MERGE_TM, MERGE_TK), lambda bb, i, k: (bb, i, gl0 + k)),
            pl.BlockSpec((d, d), lambda bb, i, k: (0, 0)),
        ],
        out_specs=pl.BlockSpec((1, MERGE_TM, d), lambda bb, i, k: (bb, i, 0)),
        out_shape=jax.ShapeDtypeStruct((b, s, d), F32),
        scratch_shapes=[pltpu.VMEM((MERGE_TM, d), F32)],
        compiler_params=_params("parallel", "parallel", "arbitrary"),
        name="merge",
    )(x, mod, attn, lru, p, p, w_out)


def _deinterleave_heads(w, n_heads):
    lead = w.shape[:-1]
    w = w.reshape(lead + (n_heads, HEAD_DIM // 2, 2))
    w = jnp.swapaxes(w, -1, -2)
    return w.reshape(lead + (n_heads * HEAD_DIM,))


def _rope_tables(n_tok):
    rows = n_tok // GRID_W
    row = jnp.repeat(jnp.arange(rows, dtype=F32), GRID_W)
    col = jnp.tile(jnp.arange(GRID_W, dtype=F32), rows)
    axis_dims = HEAD_DIM // 2
    freqs = ROPE_THETA ** (-jnp.arange(0, axis_dims, 2, dtype=F32) / axis_dims)
    ang = jnp.concatenate([row[:, None] * freqs, col[:, None] * freqs], axis=-1)
    cos, sin = jnp.cos(ang), jnp.sin(ang)
    return jnp.concatenate([cos, cos], axis=-1), jnp.concatenate([-sin, sin], axis=-1)


def kernel(x, c, ctx, c_ctx, w_mod, b_mod, norm_g, ffn_wg, ffn_wu, ffn_wd, w_in, w_out, q_norm_g,
           k_norm_g, conv_w, conv_b, lru_wa, lru_ba, lru_wx, lru_bx, lru_lambda, final_norm_g):
    b, s, d = x.shape
    n_ctx = ctx.shape[1]
    ctx_row = b

    cc = jnp.concatenate([c, c_ctx[None], jnp.zeros((SUBLANES - b - 1, d), F32)], axis=0)
    mod = _modulation(cc, w_mod[0], b_mod[0]).reshape(SUBLANES, N_MOD, d)

    wg = ffn_wg[0].astype(BF16)
    wu = ffn_wu[0].astype(BF16)
    wd = ffn_wd[0].astype(BF16)
    w_in_p = jnp.concatenate([
        _deinterleave_heads(w_in[0][:, OFF_Q:OFF_K], N_Q_HEADS),
        _deinterleave_heads(w_in[0][:, OFF_K:OFF_V], N_KV_HEADS),
        w_in[0][:, OFF_V:OFF_GA],
        0.5 * w_in[0][:, OFF_GA:],
    ], axis=1).astype(BF16)
    w_out_b = (0.5 * w_out[0]).astype(BF16)
    qkg = jnp.stack([
        _deinterleave_heads(q_norm_g[0], 1) * (HEAD_DIM ** -0.5 * LOG2_E),
        _deinterleave_heads(k_norm_g[0], 1),
    ])
    cosf, sinf = _rope_tables(s)
    w_lru = (0.5 * jnp.concatenate([lru_wa[0], lru_wx[0]], axis=-1)).astype(BF16)
    ba_h = 0.5 * lru_ba[0]
    bx_h = 0.5 * lru_bx[0]

    lat_row = lambda bb: bb
    ctx_rowf = lambda bb: ctx_row

    x1 = _ffn(x, mod, lat_row, norm_g[0, 0], wg, wu, wd, final_norm_g,
              ffn_idx=0, mod_base=0, final_norm=False, tm=512)
    ctx_flat = ctx.reshape(1, b * n_ctx, d)
    ctx1 = _ffn(ctx_flat, mod, ctx_rowf, norm_g[0, 0], wg, wu, wd, final_norm_g,
                ffn_idx=0, mod_base=0, final_norm=False, tm=512)

    pc = _proj(ctx1, mod, ctx_rowf, norm_g[0, 1], w_in_p, qkg, cosf, sinf,
               col0=PROJ_TILE_KV, ncol=3, rope=False, tm=b * n_ctx)
    pc = pc.reshape(b, n_ctx, 3 * PROJ_TN)
    zeros = jnp.zeros((b, LRU_W), F32)
    lx_c = OFF_LX - OFF_K
    _, hf0 = _lru(pc, lx_c, conv_w[0], conv_b[0], w_lru[0], ba_h[0], bx_h[0],
                  lru_lambda[0, 0], zeros, reverse=False)
    _, hb0 = _lru(pc, lx_c, conv_w[0], conv_b[0], w_lru[1], ba_h[1], bx_h[1],
                  lru_lambda[0, 1], zeros, reverse=True)

    p = _proj(x1, mod, lat_row, norm_g[0, 1], w_in_p, qkg, cosf, sinf,
              col0=0, ncol=IN_W // PROJ_TN, rope=True, tm=1024)
    attn = _attention(p, pc)
    hf, _ = _lru(p, OFF_LX, conv_w[0], conv_b[0], w_lru[0], ba_h[0], bx_h[0],
                 lru_lambda[0, 0], hf0, reverse=False)
    lru, _ = _lru(p, OFF_LX, conv_w[0], conv_b[0], w_lru[1], ba_h[1], bx_h[1],
                  lru_lambda[0, 1], hb0, hf=hf, lg_col0=OFF_LG, reverse=True)
    x2 = _merge(x1, mod, attn, lru, p, w_out_b)

    return _ffn(x2, mod, lat_row, norm_g[0, 2], wg, wu, wd, final_norm_g,
                ffn_idx=1, mod_base=6, final_norm=True, tm=512)
```

```python
import functools

import jax
import jax.numpy as jnp
from jax import lax
from jax.experimental import pallas as pl
from jax.experimental.pallas import tpu as pltpu

D_MODEL = 2048
GRID_W = 64
HEAD_DIM = 128
N_Q_HEADS = 16
N_KV_HEADS = 4
GQA_GROUP = N_Q_HEADS // N_KV_HEADS
ATTN_W = N_Q_HEADS * HEAD_DIM
KV_W = N_KV_HEADS * HEAD_DIM
LRU_W = D_MODEL
LRU_BLOCK_DIM = 128
LRU_C = 8.0
D_FF = 5632
ROPE_THETA = 10000.0
EPS = 1e-6
N_MOD = 9
FFN_RES = 0.5
LOG2_E = 1.4426950408889634
OFF_Q = 0
OFF_K = OFF_Q + ATTN_W
OFF_V = OFF_K + KV_W
OFF_LX = OFF_V + KV_W
OFF_LG = OFF_LX + LRU_W
OFF_GA = OFF_LG + LRU_W
OFF_GL = OFF_GA + D_MODEL
IN_W = OFF_GL + D_MODEL

SUBLANES = 8
LANES = 128
VMEM_LIMIT = 56 * 1024 * 1024

BF16 = jnp.bfloat16
F32 = jnp.float32


def _params(*sem):
    return pltpu.CompilerParams(dimension_semantics=sem, vmem_limit_bytes=VMEM_LIMIT)


def _dot(a, b):
    return jnp.dot(a, b, preferred_element_type=F32)


def _rms(x):
    return x * lax.rsqrt(jnp.mean(x * x, axis=-1, keepdims=True) + EPS)


ROW_CHUNK = 16


def _for_row_chunks(n_rows, body):
    def step(r, carry):
        body(pl.ds(pl.multiple_of(r * ROW_CHUNK, ROW_CHUNK), ROW_CHUNK))
        return carry
    lax.fori_loop(0, n_rows // ROW_CHUNK, step, 0, unroll=8)


def _norm_modulate(x_ref, g_ref, mod_ref, mod_base, h_scr):
    shift = mod_ref[0, mod_base:mod_base + 1, :]
    gain = g_ref[...] * (1.0 + mod_ref[0, mod_base + 1:mod_base + 2, :])

    def body(rows):
        x = x_ref[0, rows, :]
        rs = lax.rsqrt(jnp.mean(x * x, axis=-1, keepdims=True) + EPS)
        h_scr[rows, :] = (x * rs * gain + shift).astype(BF16)

    _for_row_chunks(h_scr.shape[0], body)


MOD_TN = 1024


def _mod_kernel(c_ref, w_ref, b_ref, o_ref):
    def split(x):
        hi = x.astype(BF16)
        return hi, (x - hi.astype(F32)).astype(BF16)

    rows = c_ref.shape[0]
    c = c_ref[...]
    s_hi, s_lo = split(c * jax.nn.sigmoid(c))
    w_hi, w_lo = split(w_ref[...])
    both = _dot(jnp.concatenate([s_hi, s_lo], axis=0), w_hi)
    o_ref[...] = both[:rows] + both[rows:] + _dot(s_hi, w_lo) + b_ref[...]


def _modulation(cc, w_mod, b_mod):
    rows, d = cc.shape
    n = w_mod.shape[1]
    return pl.pallas_call(
        _mod_kernel,
        grid=(n // MOD_TN,),
        in_specs=[
            pl.BlockSpec((rows, d), lambda j: (0, 0)),
            pl.BlockSpec((d, MOD_TN), lambda j: (0, j)),
            pl.BlockSpec((1, MOD_TN), lambda j: (0, j)),
        ],
        out_specs=pl.BlockSpec((rows, MOD_TN), lambda j: (0, j)),
        out_shape=jax.ShapeDtypeStruct((rows, n), F32),
        compiler_params=_params("arbitrary"),
        name="modulation",
    )(cc, w_mod, b_mod.reshape(1, n))


FFN_TF = 512


def _ffn_kernel(x_ref, mod_ref, g_ref, wg_ref, wu_ref, wd_ref, fg_ref, o_ref, h_scr, acc_scr,
                *, mod_base, final_norm):
    j = pl.program_id(2)

    @pl.when(j == 0)
    def _():
        _norm_modulate(x_ref, g_ref, mod_ref, mod_base, h_scr)
        acc_scr[...] = jnp.zeros_like(acc_scr)

    h = h_scr[...]
    gate = _dot(h, wg_ref[...])
    up = _dot(h, wu_ref[...])
    act = (gate * jax.nn.sigmoid(gate) * up).astype(BF16)
    acc_scr[...] += _dot(act, wd_ref[...])

    @pl.when(j == pl.num_programs(2) - 1)
    def _():
        res_gate = FFN_RES * mod_ref[0, mod_base + 2:mod_base + 3, :]

        def body(rows):
            xo = x_ref[0, rows, :] + res_gate * acc_scr[rows, :]
            if final_norm:
                xo = _rms(xo) * fg_ref[...]
            o_ref[0, rows, :] = xo

        _for_row_chunks(acc_scr.shape[0], body)


def _ffn(x, mod, mod_row, norm_g, wg, wu, wd, final_g, *, ffn_idx, mod_base, final_norm, tm):
    b, s, d = x.shape
    kern = functools.partial(_ffn_kernel, mod_base=mod_base, final_norm=final_norm)
    return pl.pallas_call(
        kern,
        grid=(b, s // tm, D_FF // FFN_TF),
        in_specs=[
            pl.BlockSpec((1, tm, d), lambda bb, i, j: (bb, i, 0)),
            pl.BlockSpec((1, N_MOD, d), lambda bb, i, j: (mod_row(bb), 0, 0)),
            pl.BlockSpec((1, d), lambda bb, i, j: (0, 0)),
            pl.BlockSpec((None, d, FFN_TF), lambda bb, i, j: (ffn_idx, 0, j)),
            pl.BlockSpec((None, d, FFN_TF), lambda bb, i, j: (ffn_idx, 0, j)),
            pl.BlockSpec((None, FFN_TF, d), lambda bb, i, j: (ffn_idx, j, 0)),
            pl.BlockSpec((1, d), lambda bb, i, j: (0, 0)),
        ],
        out_specs=pl.BlockSpec((1, tm, d), lambda bb, i, j: (bb, i, 0)),
        out_shape=jax.ShapeDtypeStruct((b, s, d), F32),
        scratch_shapes=[pltpu.VMEM((tm, d), BF16), pltpu.VMEM((tm, d), F32)],
        compiler_params=_params("parallel", "parallel", "arbitrary"),
        name="ffn",
    )(x, mod, norm_g.reshape(1, d), wg, wu, wd, final_g.reshape(1, d))


PROJ_TN = 1024
PROJ_TILE_KV = OFF_K // PROJ_TN


def _proj_kernel(x_ref, mod_ref, g_ref, w_ref, qkg_ref, cos_ref, sin_ref, o_ref, h_scr,
                 *, mod_base, col0, rope):
    jj = pl.program_id(2)
    j = jj + col0

    @pl.when(jj == 0)
    def _():
        _norm_modulate(x_ref, g_ref, mod_ref, mod_base, h_scr)

    z = _dot(h_scr[...], w_ref[...])

    def head_norm(t, gain):
        sq = t * t
        hi = sq.astype(BF16)
        lo = (sq - hi.astype(F32)).astype(BF16)
        mean_mat = jnp.full((2 * HEAD_DIM, HEAD_DIM), 1.0 / HEAD_DIM, BF16)
        ms = _dot(jnp.concatenate([hi, lo], axis=1), mean_mat)
        y = t * lax.rsqrt(ms + EPS) * gain
        if rope:
            y = y * cos_ref[...] + pltpu.roll(y, HEAD_DIM // 2, axis=1) * sin_ref[...]
        return y.astype(BF16)

    def store_heads(n_heads, gain):
        for hh in range(n_heads):
            sl = slice(hh * HEAD_DIM, (hh + 1) * HEAD_DIM)
            o_ref[0, :, sl] = head_norm(z[:, sl], gain)

    @pl.when(j < PROJ_TILE_KV)
    def _():
        store_heads(PROJ_TN // HEAD_DIM, qkg_ref[0:1, :])

    @pl.when(j == PROJ_TILE_KV)
    def _():
        store_heads(N_KV_HEADS, qkg_ref[1:2, :])
        o_ref[0, :, KV_W:] = z[:, KV_W:].astype(BF16)

    @pl.when(j > PROJ_TILE_KV)
    def _():
        o_ref[0] = z.astype(BF16)


def _proj(x, mod, mod_row, norm_g, w_in, qkg, cosf, sinf, *, col0, ncol, rope, tm):
    b, s, d = x.shape
    kern = functools.partial(_proj_kernel, mod_base=3, col0=col0, rope=rope)
    return pl.pallas_call(
        kern,
        grid=(b, s // tm, ncol),
        in_specs=[
            pl.BlockSpec((1, tm, d), lambda bb, i, j: (bb, i, 0)),
            pl.BlockSpec((1, N_MOD, d), lambda bb, i, j: (mod_row(bb), 0, 0)),
            pl.BlockSpec((1, d), lambda bb, i, j: (0, 0)),
            pl.BlockSpec((d, PROJ_TN), lambda bb, i, j: (0, j + col0)),
            pl.BlockSpec((2, HEAD_DIM), lambda bb, i, j: (0, 0)),
            pl.BlockSpec((tm, HEAD_DIM), lambda bb, i, j: (i, 0)),
            pl.BlockSpec((tm, HEAD_DIM), lambda bb, i, j: (i, 0)),
        ],
        out_specs=pl.BlockSpec((1, tm, PROJ_TN), lambda bb, i, j: (bb, i, j)),
        out_shape=jax.ShapeDtypeStruct((b, s, ncol * PROJ_TN), BF16),
        scratch_shapes=[pltpu.VMEM((tm, d), BF16)],
        compiler_params=_params("parallel", "parallel", "arbitrary"),
        name="proj",
    )(x, mod, norm_g.reshape(1, d), w_in, qkg, cosf, sinf)


ATTN_TQ = 512
ATTN_SUB = 256
ATTN_KEY_CHUNK = 256


def _attn_kernel(q_ref, qn_ref, k_ref, v_ref, kc_ref, vc_ref, o_ref,
                 vt_scr, vct_scr, s_scr, m_scr):
    i = pl.program_id(2)
    s_len = k_ref.shape[1]
    c_len = kc_ref.shape[1]
    nt = (((1,), (1,)), ((), ()))
    pair_w = 2 * HEAD_DIM
    chunks = [(k_ref, vt_scr, r, ATTN_KEY_CHUNK, r) for r in range(0, s_len, ATTN_KEY_CHUNK)]
    chunks.append((kc_ref, vct_scr, 0, c_len, s_len))
    units = [(r0, pr * pair_w) for pr in range(GQA_GROUP // 2)
             for r0 in range(0, q_ref.shape[1], ATTN_SUB)]

    def stack_pair(ref, row0, col0):
        rows = slice(row0, row0 + ATTN_SUB)
        return jnp.concatenate([ref[0, rows, col0:col0 + HEAD_DIM],
                                ref[0, rows, col0 + HEAD_DIM:col0 + pair_w]], axis=0)

    def fold8(x, op):
        return op(x.reshape(x.shape[0] // SUBLANES, SUBLANES, x.shape[1]), axis=0)

    def chunk_scores(q2, slot, chunk, mpart):
        kref, _, r, n, row = chunk
        s = lax.dot_general(kref[0, r:r + n, :], q2, nt, preferred_element_type=F32)
        s_scr[slot, row:row + n, :] = s
        cm = fold8(s, jnp.max)
        return cm if mpart is None else jnp.maximum(mpart, cm)

    def phase(q_next, slot_next, slot_cur, unit):
        row0, col0 = unit
        m_cur = m_scr[slot_cur]
        mpart = lpart = ot = None
        for chunk in chunks:
            _, vtref, r, n, row = chunk
            mpart = chunk_scores(q_next, slot_next, chunk, mpart)
            p = jnp.exp2(s_scr[slot_cur, row:row + n, :] - m_cur)
            cl = fold8(p, jnp.sum)
            lpart = cl if lpart is None else lpart + cl
            pv = _dot(vtref[:, r:r + n], p.astype(BF16))
            ot = pv if ot is None else ot + pv
        m_scr[slot_next] = jnp.max(mpart, axis=0, keepdims=True)
        ot = ot * (1.0 / jnp.sum(lpart, axis=0, keepdims=True))
        for e in range(2):
            o_ref[0, row0:row0 + ATTN_SUB, col0 + e * HEAD_DIM:col0 + (e + 1) * HEAD_DIM] = (
                ot[:, e * ATTN_SUB:(e + 1) * ATTN_SUB].T.astype(BF16))

    @pl.when(i == 0)
    def _():
        vt_scr[...] = v_ref[0].T
        vct_scr[...] = vc_ref[0].T
        q2 = stack_pair(q_ref, *units[0])
        mpart = None
        for chunk in chunks:
            mpart = chunk_scores(q2, 0, chunk, mpart)
        m_scr[0] = jnp.max(mpart, axis=0, keepdims=True)

    for u, unit in enumerate(units):
        if u + 1 < len(units):
            q_next = stack_pair(q_ref, *units[u + 1])
        else:
            q_next = stack_pair(qn_ref, 0, 0)
        phase(q_next, (u + 1) % 2, u % 2, unit)


def _attention(p, pc):
    b, s, _ = p.shape
    c = pc.shape[1]
    gw = GQA_GROUP * HEAD_DIM
    kb = OFF_K // HEAD_DIM
    vb = OFF_V // HEAD_DIM
    n = s // ATTN_TQ
    return pl.pallas_call(
        _attn_kernel,
        grid=(b, N_KV_HEADS, n),
        in_specs=[
            pl.BlockSpec((1, ATTN_TQ, gw), lambda bb, g, i: (bb, i, g)),
            pl.BlockSpec((1, ATTN_SUB, 2 * HEAD_DIM),
                         lambda bb, g, i: (bb, jnp.minimum(i + 1, n - 1) * (ATTN_TQ // ATTN_SUB),
                                           2 * g)),
            pl.BlockSpec((1, s, HEAD_DIM), lambda bb, g, i: (bb, 0, kb + g)),
            pl.BlockSpec((1, s, HEAD_DIM), lambda bb, g, i: (bb, 0, vb + g)),
            pl.BlockSpec((1, c, HEAD_DIM), lambda bb, g, i: (bb, 0, g)),
            pl.BlockSpec((1, c, HEAD_DIM), lambda bb, g, i: (bb, 0, N_KV_HEADS + g)),
        ],
        out_specs=pl.BlockSpec((1, ATTN_TQ, gw), lambda bb, g, i: (bb, i, g)),
        out_shape=jax.ShapeDtypeStruct((b, s, ATTN_W), BF16),
        scratch_shapes=[
            pltpu.VMEM((HEAD_DIM, s), BF16),
            pltpu.VMEM((HEAD_DIM, c), BF16),
            pltpu.VMEM((2, s + c, 2 * ATTN_SUB), F32),
            pltpu.VMEM((2, 1, 2 * ATTN_SUB), F32),
        ],
        compiler_params=_params("arbitrary", "arbitrary", "arbitrary"),
        name="attention",
    )(p, p, p, p, pc, pc)


LRU_CB = 512
LRU_TL = 256
LRU_TINY = 1e-30
LRU_GROUP = 2
HALO = SUBLANES
CONV_LEFT = 2


def _gelu_tanh(x):
    return 0.5 * x * (1.0 + jnp.tanh(0.7978845608028654 * (x + 0.044715 * x * x * x)))


def _token_conv(x_ref, xp_ref, xn_ref, cw_ref, cb_ref, ci, n, nb):
    tl, cb = x_ref.shape[1], x_ref.shape[2]
    ri = lax.broadcasted_iota(jnp.int32, (tl, tl), 0)
    cj = lax.broadcasted_iota(jnp.int32, (tl, tl), 1)
    shifts = jnp.concatenate(
        [(cj == ri + d).astype(BF16) for d in (-2, -1, 1)], axis=0)
    row8 = lax.broadcasted_iota(jnp.int32, (HALO, cb), 0)
    has_prev = (ci > 0).astype(F32)
    has_next = (ci < n - 1).astype(F32)
    w0, w1, w2, w3 = (cw_ref[k:k + 1, :] for k in range(4))
    xcs = []
    for bb in range(nb):
        xb = x_ref[bb]
        sh = _dot(shifts, xb)
        y = (cb_ref[...] + w2 * xb.astype(F32) + w0 * sh[:tl] + w1 * sh[tl:2 * tl]
             + w3 * sh[2 * tl:])
        prev = xp_ref[bb].astype(F32) * has_prev
        nxt = xn_ref[bb].astype(F32) * has_next
        head = (w0 * jnp.where(row8 < 2, pltpu.roll(prev, 2, axis=0), 0.0)
                + w1 * jnp.where(row8 < 1, pltpu.roll(prev, 1, axis=0), 0.0))
        tail = w3 * jnp.where(row8 == HALO - 1, pltpu.roll(nxt, HALO - 1, axis=0), 0.0)
        xcs.append(jnp.concatenate(
            [y[:HALO] + head, y[HALO:tl - HALO], y[tl - HALO:] + tail], axis=0))
    return jnp.concatenate(xcs, axis=0)


def _lru_kernel(*refs, reverse, gated, nb, conv_from):
    refs = list(refs)
    if conv_from == "given":
        xc_ref = refs.pop(0)
        tl, cb = xc_ref.shape[1], xc_ref.shape[2]
    else:
        x_ref, xp_ref, xn_ref, cw_ref, cb_ref = refs[:5]
        del refs[:5]
        tl, cb = x_ref.shape[1], x_ref.shape[2]
    w_ref, ba_ref, bx_ref, lam_ref, h0_ref = refs[:5]
    del refs[:5]
    if gated:
        hf_ref, lg_ref = refs[:2]
        del refs[:2]
    o_ref, hl_ref = refs[:2]
    del refs[:2]
    if conv_from == "input+emit":
        xc_out_ref = refs.pop(0)
    a_scr, u_scr, ho_scr, h_scr = refs
    i = pl.program_id(1)
    n = pl.num_programs(1)
    ci = n - 1 - i if reverse else i

    def slot_rows(blk, bb):
        return pl.ds((blk % LRU_GROUP) * nb + bb, tl, stride=SUBLANES)

    @pl.when(i == 0)
    def _():
        for blk in range(cb // LRU_BLOCK_DIM):
            e = blk % LRU_GROUP
            h_scr[blk // LRU_GROUP, e * nb:(e + 1) * nb, :] = (
                h0_ref[:, blk * LRU_BLOCK_DIM:(blk + 1) * LRU_BLOCK_DIM])

    if conv_from == "given":
        xc = xc_ref[...].reshape(nb * tl, cb)
    else:
        xc = _token_conv(x_ref, xp_ref, xn_ref, cw_ref, cb_ref, ci, n, nb)
        if conv_from == "input+emit":
            xc_out_ref[...] = xc.reshape(nb, tl, cb)

    lam = lam_ref[...]
    neg = -lam
    sp = jnp.maximum(neg, 0.0) + jnp.log1p(jnp.exp(-jnp.abs(neg)))
    half_c = (-0.5 * LRU_C * LOG2_E) * sp
    for blk in range(cb // LRU_BLOCK_DIM):
        sl = slice(blk * LRU_BLOCK_DIM, (blk + 1) * LRU_BLOCK_DIM)
        xb = xc[:, sl]
        z = _dot(xb.astype(BF16), w_ref[blk])
        tr = jnp.tanh(z[:, :LRU_BLOCK_DIM] + ba_ref[:, sl])
        ti = jnp.tanh(z[:, LRU_BLOCK_DIM:] + bx_ref[:, sl])
        hc = half_c[:, sl]
        a = jnp.exp2(hc + hc * tr)
        hx = 0.5 * xb
        y = 1.0 - a * a
        root = y * lax.rsqrt(jnp.maximum(y, LRU_TINY))
        u = root * (hx + hx * ti)
        for bb in range(nb):
            a_scr[blk // LRU_GROUP, slot_rows(blk, bb), :] = a[bb * tl:(bb + 1) * tl]
            u_scr[blk // LRU_GROUP, slot_rows(blk, bb), :] = u[bb * tl:(bb + 1) * tl]

    ngrp = cb // LRU_BLOCK_DIM // LRU_GROUP

    def step(t, hs):
        t1 = tl - 1 - 2 * t if reverse else 2 * t
        t2 = t1 - 1 if reverse else t1 + 1
        rows1 = pl.ds(pl.multiple_of(t1 * SUBLANES, SUBLANES), SUBLANES)
        rows2 = pl.ds(pl.multiple_of(t2 * SUBLANES, SUBLANES), SUBLANES)
        out = []
        for g in range(ngrp):
            a1 = a_scr[g, rows1, :]
            u1 = u_scr[g, rows1, :]
            a2 = a_scr[g, rows2, :]
            u2 = u_scr[g, rows2, :]
            h2 = (a2 * a1) * hs[g] + (a2 * u1 + u2)
            ho_scr[g, rows1, :] = a1 * hs[g] + u1
            ho_scr[g, rows2, :] = h2
            out.append(h2)
        return tuple(out)

    hs = lax.fori_loop(0, tl // 2, step, tuple(h_scr[g] for g in range(ngrp)), unroll=4)
    for g in range(ngrp):
        h_scr[g] = hs[g]
    for blk in range(cb // LRU_BLOCK_DIM):
        e = blk % LRU_GROUP
        hl_ref[:, blk * LRU_BLOCK_DIM:(blk + 1) * LRU_BLOCK_DIM] = (
            hs[blk // LRU_GROUP][e * nb:(e + 1) * nb])

    for bb in range(nb):
        hb = jnp.concatenate(
            [ho_scr[blk // LRU_GROUP, slot_rows(blk, bb), :]
             for blk in range(cb // LRU_BLOCK_DIM)], axis=-1)
        if gated:
            hb = (hb + hf_ref[bb].astype(F32)) * _gelu_tanh(lg_ref[bb].astype(F32))
        o_ref[bb] = hb.astype(BF16)


def _lru(p, x_col0, conv_w, conv_b, w_blk, ba, bx, lam, h0, *, reverse, emit_conv=False,
         conv=None, hf=None, lg_col0=None):
    nb, t, _ = p.shape
    tl = min(LRU_TL, t)
    n = t // tl
    cb = LRU_CB
    xb0 = x_col0 // cb
    gated = hf is not None
    hpc = tl // HALO
    assert nb * LRU_GROUP == SUBLANES
    ngrp = cb // LRU_BLOCK_DIM // LRU_GROUP
    conv_from = "given" if conv is not None else ("input+emit" if emit_conv else "input")

    def pos(i):
        return n - 1 - i if reverse else i

    chunk_spec = lambda col0: pl.BlockSpec((nb, tl, cb), lambda c, i: (0, pos(i), col0 + c))
    row_spec = lambda rows: pl.BlockSpec((rows, cb), lambda c, i: (0, c))
    if conv is not None:
        in_specs = [chunk_spec(0)]
        args = [conv]
    else:
        in_specs = [
            chunk_spec(xb0),
            pl.BlockSpec((nb, HALO, cb),
                         lambda c, i: (0, jnp.maximum(pos(i) * hpc - 1, 0), xb0 + c)),
            pl.BlockSpec((nb, HALO, cb),
                         lambda c, i: (0, jnp.minimum((pos(i) + 1) * hpc, t // HALO - 1), xb0 + c)),
            row_spec(4),
            row_spec(1),
        ]
        args = [p, p, p, conv_w, conv_b.reshape(1, -1)]
    in_specs += [
        pl.BlockSpec((cb // LRU_BLOCK_DIM, LRU_BLOCK_DIM, 2 * LRU_BLOCK_DIM), lambda c, i: (c, 0, 0)),
        row_spec(1), row_spec(1), row_spec(1), row_spec(nb),
    ]
    args += [w_blk, ba.reshape(1, -1), bx.reshape(1, -1), lam.reshape(1, -1), h0]
    if gated:
        in_specs += [chunk_spec(0), chunk_spec(lg_col0 // cb)]
        args += [hf, p]
    out_specs = [chunk_spec(0), row_spec(nb)]
    out_shape = [jax.ShapeDtypeStruct((nb, t, LRU_W), BF16), jax.ShapeDtypeStruct((nb, LRU_W), F32)]
    if emit_conv:
        out_specs.append(chunk_spec(0))
        out_shape.append(jax.ShapeDtypeStruct((nb, t, LRU_W), F32))
    kern = functools.partial(_lru_kernel, reverse=reverse, gated=gated, nb=nb, conv_from=conv_from)
    return pl.pallas_call(
        kern,
        grid=(LRU_W // cb, n),
        in_specs=in_specs,
        out_specs=out_specs,
        out_shape=out_shape,
        scratch_shapes=[
            pltpu.VMEM((ngrp, tl * SUBLANES, LRU_BLOCK_DIM), F32),
            pltpu.VMEM((ngrp, tl * SUBLANES, LRU_BLOCK_DIM), F32),
            pltpu.VMEM((ngrp, tl * SUBLANES, LRU_BLOCK_DIM), F32),
            pltpu.VMEM((ngrp, SUBLANES, LRU_BLOCK_DIM), F32),
        ],
        compiler_params=_params("parallel", "arbitrary"),
        name="lru_bwd" if reverse else "lru_fwd",
    )(*args)


MERGE_TM = 512
MERGE_TK = 1024


def _merge_kernel(x_ref, mod_ref, attn_ref, lru_ref, ga_ref, gl_ref, w_ref, o_ref, acc_scr):
    kk = pl.program_id(2)

    @pl.when(kk == 0)
    def _():
        acc_scr[...] = jnp.zeros_like(acc_scr)

    m = (attn_ref[0] * (1.0 + jnp.tanh(ga_ref[0])) + lru_ref[0] * (1.0 + jnp.tanh(gl_ref[0])))
    w_rows = pl.ds(pl.multiple_of(kk * MERGE_TK, MERGE_TK), MERGE_TK)
    acc_scr[...] += _dot(m, w_ref[w_rows, :])

    @pl.when(kk == pl.num_programs(2) - 1)
    def _():
        gate = mod_ref[0, 5:6, :]

        def body(rows):
            o_ref[0, rows, :] = x_ref[0, rows, :] + gate * acc_scr[rows, :]

        _for_row_chunks(acc_scr.shape[0], body)


def _merge(x, mod, attn, lru, p, w_out):
    b, s, d = x.shape
    ga0 = OFF_GA // MERGE_TK
    gl0 = OFF_GL // MERGE_TK
    return pl.pallas_call(
        _merge_kernel,
        grid=(b, s // MERGE_TM, d // MERGE_TK),
        in_specs=[
            pl.BlockSpec((1, MERGE_TM, d), lambda bb, i, k: (bb, i, 0)),
            pl.BlockSpec((1, N_MOD, d), lambda bb, i, k: (bb, 0, 0)),
            pl.BlockSpec((1, MERGE_TM, MERGE_TK), lambda bb, i, k: (bb, i, k)),
            pl.BlockSpec((1, MERGE_TM, MERGE_TK), lambda bb, i, k: (bb, i, k)),
            pl.BlockSpec((1, MERGE_TM, MERGE_TK), lambda bb, i, k: (bb, i, ga0 + k)),
            pl.BlockSpec((1, MERGE_TM, MERGE_TK), lambda bb, i, k: (bb, i, gl0 + k)),
            pl.BlockSpec((d, d), lambda bb, i, k: (0, 0)),
        ],
        out_specs=pl.BlockSpec((1, MERGE_TM, d), lambda bb, i, k: (bb, i, 0)),
        out_shape=jax.ShapeDtypeStruct((b, s, d), F32),
        scratch_shapes=[pltpu.VMEM((MERGE_TM, d), F32)],
        compiler_params=_params("parallel", "parallel", "arbitrary"),
        name="merge",
    )(x, mod, attn, lru, p, p, w_out)


def _deinterleave_heads(w, n_heads):
    lead = w.shape[:-1]
    w = w.reshape(lead + (n_heads, HEAD_DIM // 2, 2))
    w = jnp.swapaxes(w, -1, -2)
    return w.reshape(lead + (n_heads * HEAD_DIM,))


def _rope_tables(n_tok):
    rows = n_tok // GRID_W
    row = jnp.repeat(jnp.arange(rows, dtype=F32), GRID_W)
    col = jnp.tile(jnp.arange(GRID_W, dtype=F32), rows)
    axis_dims = HEAD_DIM // 2
    freqs = ROPE_THETA ** (-jnp.arange(0, axis_dims, 2, dtype=F32) / axis_dims)
    ang = jnp.concatenate([row[:, None] * freqs, col[:, None] * freqs], axis=-1)
    cos, sin = jnp.cos(ang), jnp.sin(ang)
    return jnp.concatenate([cos, cos], axis=-1), jnp.concatenate([-sin, sin], axis=-1)


def kernel(x, c, ctx, c_ctx, w_mod, b_mod, norm_g, ffn_wg, ffn_wu, ffn_wd, w_in, w_out, q_norm_g,
           k_norm_g, conv_w, conv_b, lru_wa, lru_ba, lru_wx, lru_bx, lru_lambda, final_norm_g):
    b, s, d = x.shape
    n_ctx = ctx.shape[1]
    ctx_row = b

    cc = jnp.concatenate([c, c_ctx[None], jnp.zeros((SUBLANES - b - 1, d), F32)], axis=0)
    mod = _modulation(cc, w_mod[0], b_mod[0]).reshape(SUBLANES, N_MOD, d)

    wg = ffn_wg[0].astype(BF16)
    wu = ffn_wu[0].astype(BF16)
    wd = ffn_wd[0].astype(BF16)
    w_in_p = jnp.concatenate([
        _deinterleave_heads(w_in[0][:, OFF_Q:OFF_K], N_Q_HEADS),
        _deinterleave_heads(w_in[0][:, OFF_K:OFF_V], N_KV_HEADS),
        w_in[0][:, OFF_V:OFF_GA],
        0.5 * w_in[0][:, OFF_GA:],
    ], axis=1).astype(BF16)
    w_out_b = (0.5 * w_out[0]).astype(BF16)
    qkg = jnp.stack([
        _deinterleave_heads(q_norm_g[0], 1) * (HEAD_DIM ** -0.5 * LOG2_E),
        _deinterleave_heads(k_norm_g[0], 1),
    ])
    cosf, sinf = _rope_tables(s)
    w_lru = (0.5 * jnp.concatenate([lru_wa[0], lru_wx[0]], axis=-1)).astype(BF16)
    ba_h = 0.5 * lru_ba[0]
    bx_h = 0.5 * lru_bx[0]

    lat_row = lambda bb: bb
    ctx_rowf = lambda bb: ctx_row

    x1 = _ffn(x, mod, lat_row, norm_g[0, 0], wg, wu, wd, final_norm_g,
              ffn_idx=0, mod_base=0, final_norm=False, tm=512)
    ctx_flat = ctx.reshape(1, b * n_ctx, d)
    ctx1 = _ffn(ctx_flat, mod, ctx_rowf, norm_g[0, 0], wg, wu, wd, final_norm_g,
                ffn_idx=0, mod_base=0, final_norm=False, tm=512)

    pc = _proj(ctx1, mod, ctx_rowf, norm_g[0, 1], w_in_p, qkg, cosf, sinf,
               col0=PROJ_TILE_KV, ncol=3, rope=False, tm=b * n_ctx)
    pc = pc.reshape(b, n_ctx, 3 * PROJ_TN)
    zeros = jnp.zeros((b, LRU_W), F32)
    lx_c = OFF_LX - OFF_K
    lru_f = (w_lru[0], ba_h[0], bx_h[0], lru_lambda[0, 0])
    lru_b = (w_lru[1], ba_h[1], bx_h[1], lru_lambda[0, 1])
    _, hf0 = _lru(pc, lx_c, conv_w[0], conv_b[0], *lru_f, zeros, reverse=False)
    _, hb0 = _lru(pc, lx_c, conv_w[0], conv_b[0], *lru_b, zeros, reverse=True)

    p = _proj(x1, mod, lat_row, norm_g[0, 1], w_in_p, qkg, cosf, sinf,
              col0=0, ncol=IN_W // PROJ_TN, rope=True, tm=1024)
    attn = _attention(p, pc)
    hf, _, xc = _lru(p, OFF_LX, conv_w[0], conv_b[0], *lru_f, hf0, reverse=False, emit_conv=True)
    lru, _ = _lru(p, OFF_LX, conv_w[0], conv_b[0], *lru_b, hb0, reverse=True, conv=xc,
                  hf=hf, lg_col0=OFF_LG)
    x2 = _merge(x1, mod, attn, lru, p, w_out_b)

    return _ffn(x2, mod, lat_row, norm_g[0, 2], wg, wu, wd, final_norm_g,
                ffn_idx=1, mod_base=6, final_norm=True, tm=512)
```

```python
import functools

import jax
import jax.numpy as jnp
from jax import lax
from jax.experimental import pallas as pl
from jax.experimental.pallas import tpu as pltpu

D_MODEL = 2048
GRID_W = 64
HEAD_DIM = 128
N_Q_HEADS = 16
N_KV_HEADS = 4
GQA_GROUP = N_Q_HEADS // N_KV_HEADS
ATTN_W = N_Q_HEADS * HEAD_DIM
KV_W = N_KV_HEADS * HEAD_DIM
LRU_W = D_MODEL
LRU_BLOCK_DIM = 128
LRU_C = 8.0
D_FF = 5632
ROPE_THETA = 10000.0
EPS = 1e-6
N_MOD = 9
FFN_RES = 0.5
LOG2_E = 1.4426950408889634
OFF_Q = 0
OFF_K = OFF_Q + ATTN_W
OFF_V = OFF_K + KV_W
OFF_LX = OFF_V + KV_W
OFF_LG = OFF_LX + LRU_W
OFF_GA = OFF_LG + LRU_W
OFF_GL = OFF_GA + D_MODEL
IN_W = OFF_GL + D_MODEL

SUBLANES = 8
LANES = 128
VMEM_LIMIT = 56 * 1024 * 1024

BF16 = jnp.bfloat16
F32 = jnp.float32


def _params(*sem):
    return pltpu.CompilerParams(dimension_semantics=sem, vmem_limit_bytes=VMEM_LIMIT)


def _dot(a, b):
    return jnp.dot(a, b, preferred_element_type=F32)


def _rms(x):
    return x * lax.rsqrt(jnp.mean(x * x, axis=-1, keepdims=True) + EPS)


ROW_CHUNK = 16


def _for_row_chunks(n_rows, body):
    def step(r, carry):
        body(pl.ds(pl.multiple_of(r * ROW_CHUNK, ROW_CHUNK), ROW_CHUNK))
        return carry
    lax.fori_loop(0, n_rows // ROW_CHUNK, step, 0, unroll=8)


def _norm_modulate(load_rows, g_ref, mod_ref, mod_base, h_scr):
    shift = mod_ref[0, mod_base:mod_base + 1, :]
    gain = g_ref[...] * (1.0 + mod_ref[0, mod_base + 1:mod_base + 2, :])

    def body(rows):
        x = load_rows(rows)
        rs = lax.rsqrt(jnp.mean(x * x, axis=-1, keepdims=True) + EPS)
        h_scr[rows, :] = (x * rs * gain + shift).astype(BF16)

    _for_row_chunks(h_scr.shape[0], body)


MOD_TN = 1024


def _mod_kernel(c_ref, w_ref, b_ref, o_ref):
    def split(x):
        hi = x.astype(BF16)
        return hi, (x - hi.astype(F32)).astype(BF16)

    rows = c_ref.shape[0]
    c = c_ref[...]
    s_hi, s_lo = split(c * jax.nn.sigmoid(c))
    w_hi, w_lo = split(w_ref[...])
    both = _dot(jnp.concatenate([s_hi, s_lo], axis=0), w_hi)
    o_ref[...] = both[:rows] + both[rows:] + _dot(s_hi, w_lo) + b_ref[...]


def _modulation(cc, w_mod, b_mod):
    rows, d = cc.shape
    n = w_mod.shape[1]
    return pl.pallas_call(
        _mod_kernel,
        grid=(n // MOD_TN,),
        in_specs=[
            pl.BlockSpec((rows, d), lambda j: (0, 0)),
            pl.BlockSpec((d, MOD_TN), lambda j: (0, j)),
            pl.BlockSpec((1, MOD_TN), lambda j: (0, j)),
        ],
        out_specs=pl.BlockSpec((rows, MOD_TN), lambda j: (0, j)),
        out_shape=jax.ShapeDtypeStruct((rows, n), F32),
        compiler_params=_params("arbitrary"),
        name="modulation",
    )(cc, w_mod, b_mod.reshape(1, n))


FFN_TF = 512


def _ffn_kernel(*refs, mod_base, final_norm, has_delta):
    if has_delta:
        x_ref, d_ref, *refs = refs
    else:
        x_ref, *refs = refs
    mod_ref, g_ref, wg_ref, wu_ref, wd_ref, fg_ref, o_ref, h_scr, acc_scr = refs
    j = pl.program_id(2)

    def load_rows(rows):
        x = x_ref[0, rows, :]
        return x + d_ref[0, rows, :].astype(F32) if has_delta else x

    @pl.when(j == 0)
    def _():
        _norm_modulate(load_rows, g_ref, mod_ref, mod_base, h_scr)
        acc_scr[...] = jnp.zeros_like(acc_scr)

    h = h_scr[...]
    gate = _dot(h, wg_ref[...])
    up = _dot(h, wu_ref[...])
    act = (gate * jax.nn.sigmoid(gate) * up).astype(BF16)
    acc_scr[...] += _dot(act, wd_ref[...])

    @pl.when(j == pl.num_programs(2) - 1)
    def _():
        res_gate = FFN_RES * mod_ref[0, mod_base + 2:mod_base + 3, :]

        def body(rows):
            xo = load_rows(rows) + res_gate * acc_scr[rows, :]
            if final_norm:
                xo = _rms(xo) * fg_ref[...]
            o_ref[0, rows, :] = xo

        _for_row_chunks(acc_scr.shape[0], body)


def _ffn(x, mod, mod_row, norm_g, wg, wu, wd, final_g, *, ffn_idx, mod_base, final_norm, tm,
         delta=None):
    b, s, d = x.shape
    kern = functools.partial(_ffn_kernel, mod_base=mod_base, final_norm=final_norm,
                             has_delta=delta is not None)
    tile_spec = pl.BlockSpec((1, tm, d), lambda bb, i, j: (bb, i, 0))
    acts = [x] if delta is None else [x, delta]
    return pl.pallas_call(
        kern,
        grid=(b, s // tm, D_FF // FFN_TF),
        in_specs=[tile_spec] * len(acts) + [
            pl.BlockSpec((1, N_MOD, d), lambda bb, i, j: (mod_row(bb), 0, 0)),
            pl.BlockSpec((1, d), lambda bb, i, j: (0, 0)),
            pl.BlockSpec((None, d, FFN_TF), lambda bb, i, j: (ffn_idx, 0, j)),
            pl.BlockSpec((None, d, FFN_TF), lambda bb, i, j: (ffn_idx, 0, j)),
            pl.BlockSpec((None, FFN_TF, d), lambda bb, i, j: (ffn_idx, j, 0)),
            pl.BlockSpec((1, d), lambda bb, i, j: (0, 0)),
        ],
        out_specs=pl.BlockSpec((1, tm, d), lambda bb, i, j: (bb, i, 0)),
        out_shape=jax.ShapeDtypeStruct((b, s, d), F32),
        scratch_shapes=[pltpu.VMEM((tm, d), BF16), pltpu.VMEM((tm, d), F32)],
        compiler_params=_params("parallel", "parallel", "arbitrary"),
        name="ffn",
    )(*acts, mod, norm_g.reshape(1, d), wg, wu, wd, final_g.reshape(1, d))


PROJ_TN = 1024
PROJ_TILE_KV = OFF_K // PROJ_TN


def _proj_kernel(x_ref, mod_ref, g_ref, w_ref, qkg_ref, cos_ref, sin_ref, o_ref, h_scr,
                 *, mod_base, col0, rope):
    jj = pl.program_id(2)
    j = jj + col0

    @pl.when(jj == 0)
    def _():
        _norm_modulate(lambda rows: x_ref[0, rows, :], g_ref, mod_ref, mod_base, h_scr)

    z = _dot(h_scr[...], w_ref[...])

    def head_norm(t, gain):
        sq = t * t
        hi = sq.astype(BF16)
        lo = (sq - hi.astype(F32)).astype(BF16)
        mean_mat = jnp.full((2 * HEAD_DIM, HEAD_DIM), 1.0 / HEAD_DIM, BF16)
        ms = _dot(jnp.concatenate([hi, lo], axis=1), mean_mat)
        y = t * lax.rsqrt(ms + EPS) * gain
        if rope:
            y = y * cos_ref[...] + pltpu.roll(y, HEAD_DIM // 2, axis=1) * sin_ref[...]
        return y.astype(BF16)

    def store_heads(n_heads, gain):
        for hh in range(n_heads):
            sl = slice(hh * HEAD_DIM, (hh + 1) * HEAD_DIM)
            o_ref[0, :, sl] = head_norm(z[:, sl], gain)

    @pl.when(j < PROJ_TILE_KV)
    def _():
        store_heads(PROJ_TN // HEAD_DIM, qkg_ref[0:1, :])

    @pl.when(j == PROJ_TILE_KV)
    def _():
        store_heads(N_KV_HEADS, qkg_ref[1:2, :])
        o_ref[0, :, KV_W:] = z[:, KV_W:].astype(BF16)

    @pl.when(j > PROJ_TILE_KV)
    def _():
        o_ref[0] = z.astype(BF16)


def _proj(x, mod, mod_row, norm_g, w_in, qkg, cosf, sinf, *, col0, ncol, rope, tm):
    b, s, d = x.shape
    kern = functools.partial(_proj_kernel, mod_base=3, col0=col0, rope=rope)
    return pl.pallas_call(
        kern,
        grid=(b, s // tm, ncol),
        in_specs=[
            pl.BlockSpec((1, tm, d), lambda bb, i, j: (bb, i, 0)),
            pl.BlockSpec((1, N_MOD, d), lambda bb, i, j: (mod_row(bb), 0, 0)),
            pl.BlockSpec((1, d), lambda bb, i, j: (0, 0)),
            pl.BlockSpec((d, PROJ_TN), lambda bb, i, j: (0, j + col0)),
            pl.BlockSpec((2, HEAD_DIM), lambda bb, i, j: (0, 0)),
            pl.BlockSpec((tm, HEAD_DIM), lambda bb, i, j: (i, 0)),
            pl.BlockSpec((tm, HEAD_DIM), lambda bb, i, j: (i, 0)),
        ],
        out_specs=pl.BlockSpec((1, tm, PROJ_TN), lambda bb, i, j: (bb, i, j)),
        out_shape=jax.ShapeDtypeStruct((b, s, ncol * PROJ_TN), BF16),
        scratch_shapes=[pltpu.VMEM((tm, d), BF16)],
        compiler_params=_params("parallel", "parallel", "arbitrary"),
        name="proj",
    )(x, mod, norm_g.reshape(1, d), w_in, qkg, cosf, sinf)


PREP_TR = 1024
PROJ_TILE_GA = OFF_GA // PROJ_TN


def _prep_w_in_kernel(w_ref, o_ref):
    j = pl.program_id(1)
    half = HEAD_DIM // 2
    src = lax.broadcasted_iota(jnp.int32, (HEAD_DIM, HEAD_DIM), 0)
    dst = lax.broadcasted_iota(jnp.int32, (HEAD_DIM, HEAD_DIM), 1)
    perm = (src == jnp.where(dst < half, 2 * dst, 2 * (dst - half) + 1)).astype(BF16)

    def reorder_heads(n_heads):
        for hh in range(n_heads):
            sl = slice(hh * HEAD_DIM, (hh + 1) * HEAD_DIM)
            o_ref[:, sl] = _dot(w_ref[:, sl].astype(BF16), perm).astype(BF16)

    @pl.when(j < PROJ_TILE_KV)
    def _():
        reorder_heads(PROJ_TN // HEAD_DIM)

    @pl.when(j == PROJ_TILE_KV)
    def _():
        reorder_heads(N_KV_HEADS)
        o_ref[:, KV_W:] = w_ref[:, KV_W:].astype(BF16)

    @pl.when((j > PROJ_TILE_KV) & (j < PROJ_TILE_GA))
    def _():
        o_ref[...] = w_ref[...].astype(BF16)

    @pl.when(j >= PROJ_TILE_GA)
    def _():
        o_ref[...] = (0.5 * w_ref[...]).astype(BF16)


def _prep_w_in(w_in):
    d, n = w_in.shape
    return pl.pallas_call(
        _prep_w_in_kernel,
        grid=(d // PREP_TR, n // PROJ_TN),
        in_specs=[pl.BlockSpec((PREP_TR, PROJ_TN), lambda i, j: (i, j))],
        out_specs=pl.BlockSpec((PREP_TR, PROJ_TN), lambda i, j: (i, j)),
        out_shape=jax.ShapeDtypeStruct((d, n), BF16),
        compiler_params=_params("parallel", "parallel"),
        name="prep_w_in",
    )(w_in)


ATTN_TQ = 512
ATTN_SUB = 256
ATTN_KEY_CHUNK = 256


def _attn_kernel(q_ref, qn_ref, k_ref, v_ref, kc_ref, vc_ref, o_ref,
                 vt_scr, vct_scr, s_scr, m_scr):
    i = pl.program_id(2)
    s_len = k_ref.shape[1]
    c_len = kc_ref.shape[1]
    nt = (((1,), (1,)), ((), ()))
    pair_w = 2 * HEAD_DIM
    chunks = [(k_ref, vt_scr, r, ATTN_KEY_CHUNK, r) for r in range(0, s_len, ATTN_KEY_CHUNK)]
    chunks.append((kc_ref, vct_scr, 0, c_len, s_len))
    units = [(r0, pr * pair_w) for pr in range(GQA_GROUP // 2)
             for r0 in range(0, q_ref.shape[1], ATTN_SUB)]

    def stack_pair(ref, row0, col0):
        rows = slice(row0, row0 + ATTN_SUB)
        return jnp.concatenate([ref[0, rows, col0:col0 + HEAD_DIM],
                                ref[0, rows, col0 + HEAD_DIM:col0 + pair_w]], axis=0)

    def fold8(x, op):
        return op(x.reshape(x.shape[0] // SUBLANES, SUBLANES, x.shape[1]), axis=0)

    def chunk_scores(q2, slot, chunk, mpart):
        kref, _, r, n, row = chunk
        s = lax.dot_general(kref[0, r:r + n, :], q2, nt, preferred_element_type=F32)
        s_scr[slot, row:row + n, :] = s
        cm = fold8(s, jnp.max)
        return cm if mpart is None else jnp.maximum(mpart, cm)

    def phase(q_next, slot_next, slot_cur, unit):
        row0, col0 = unit
        m_cur = m_scr[slot_cur]
        mpart = lpart = ot = None
        for chunk in chunks:
            _, vtref, r, n, row = chunk
            mpart = chunk_scores(q_next, slot_next, chunk, mpart)
            p = jnp.exp2(s_scr[slot_cur, row:row + n, :] - m_cur)
            cl = fold8(p, jnp.sum)
            lpart = cl if lpart is None else lpart + cl
            pv = _dot(vtref[:, r:r + n], p.astype(BF16))
            ot = pv if ot is None else ot + pv
        m_scr[slot_next] = jnp.max(mpart, axis=0, keepdims=True)
        ot = ot * (1.0 / jnp.sum(lpart, axis=0, keepdims=True))
        for e in range(2):
            o_ref[0, row0:row0 + ATTN_SUB, col0 + e * HEAD_DIM:col0 + (e + 1) * HEAD_DIM] = (
                ot[:, e * ATTN_SUB:(e + 1) * ATTN_SUB].T.astype(BF16))

    @pl.when(i == 0)
    def _():
        vt_scr[...] = v_ref[0].T
        vct_scr[...] = vc_ref[0].T
        q2 = stack_pair(q_ref, *units[0])
        mpart = None
        for chunk in chunks:
            mpart = chunk_scores(q2, 0, chunk, mpart)
        m_scr[0] = jnp.max(mpart, axis=0, keepdims=True)

    for u, unit in enumerate(units):
        if u + 1 < len(units):
            q_next = stack_pair(q_ref, *units[u + 1])
        else:
            q_next = stack_pair(qn_ref, 0, 0)
        phase(q_next, (u + 1) % 2, u % 2, unit)


def _attention(p, pc):
    b, s, _ = p.shape
    c = pc.shape[1]
    gw = GQA_GROUP * HEAD_DIM
    kb = OFF_K // HEAD_DIM
    vb = OFF_V // HEAD_DIM
    n = s // ATTN_TQ
    return pl.pallas_call(
        _attn_kernel,
        grid=(b, N_KV_HEADS, n),
        in_specs=[
            pl.BlockSpec((1, ATTN_TQ, gw), lambda bb, g, i: (bb, i, g)),
            pl.BlockSpec((1, ATTN_SUB, 2 * HEAD_DIM),
                         lambda bb, g, i: (bb, jnp.minimum(i + 1, n - 1) * (ATTN_TQ // ATTN_SUB),
                                           2 * g)),
            pl.BlockSpec((1, s, HEAD_DIM), lambda bb, g, i: (bb, 0, kb + g)),
            pl.BlockSpec((1, s, HEAD_DIM), lambda bb, g, i: (bb, 0, vb + g)),
            pl.BlockSpec((1, c, HEAD_DIM), lambda bb, g, i: (bb, 0, g)),
            pl.BlockSpec((1, c, HEAD_DIM), lambda bb, g, i: (bb, 0, N_KV_HEADS + g)),
        ],
        out_specs=pl.BlockSpec((1, ATTN_TQ, gw), lambda bb, g, i: (bb, i, g)),
        out_shape=jax.ShapeDtypeStruct((b, s, ATTN_W), BF16),
        scratch_shapes=[
            pltpu.VMEM((HEAD_DIM, s), BF16),
            pltpu.VMEM((HEAD_DIM, c), BF16),
            pltpu.VMEM((2, s + c, 2 * ATTN_SUB), F32),
            pltpu.VMEM((2, 1, 2 * ATTN_SUB), F32),
        ],
        compiler_params=_params("arbitrary", "arbitrary", "arbitrary"),
        name="attention",
    )(p, p, p, p, pc, pc)


LRU_CB = 512
LRU_TL = 256
LRU_TINY = 1e-30
LRU_GROUP = 2
HALO = SUBLANES
CONV_LEFT = 2


def _gelu_tanh(x):
    return 0.5 * x * (1.0 + jnp.tanh(0.7978845608028654 * (x + 0.044715 * x * x * x)))


def _token_conv(x_ref, xp_ref, xn_ref, cw_ref, cb_ref, ci, n, nb):
    tl, cb = x_ref.shape[1], x_ref.shape[2]
    ri = lax.broadcasted_iota(jnp.int32, (tl, tl), 0)
    cj = lax.broadcasted_iota(jnp.int32, (tl, tl), 1)
    shifts = jnp.concatenate(
        [(cj == ri + d).astype(BF16) for d in (-2, -1, 1)], axis=0)
    row8 = lax.broadcasted_iota(jnp.int32, (HALO, cb), 0)
    has_prev = (ci > 0).astype(F32)
    has_next = (ci < n - 1).astype(F32)
    w0, w1, w2, w3 = (cw_ref[k:k + 1, :] for k in range(4))
    xcs = []
    for bb in range(nb):
        xb = x_ref[bb]
        sh = _dot(shifts, xb)
        y = (cb_ref[...] + w2 * xb.astype(F32) + w0 * sh[:tl] + w1 * sh[tl:2 * tl]
             + w3 * sh[2 * tl:])
        prev = xp_ref[bb].astype(F32) * has_prev
        nxt = xn_ref[bb].astype(F32) * has_next
        head = (w0 * jnp.where(row8 < 2, pltpu.roll(prev, 2, axis=0), 0.0)
                + w1 * jnp.where(row8 < 1, pltpu.roll(prev, 1, axis=0), 0.0))
        tail = w3 * jnp.where(row8 == HALO - 1, pltpu.roll(nxt, HALO - 1, axis=0), 0.0)
        xcs.append(jnp.concatenate(
            [y[:HALO] + head, y[HALO:tl - HALO], y[tl - HALO:] + tail], axis=0))
    return jnp.concatenate(xcs, axis=0)


def _lru_kernel(*refs, reverse, gated, nb, conv_from):
    refs = list(refs)
    if conv_from == "given":
        xc_ref = refs.pop(0)
        tl, cb = xc_ref.shape[1], xc_ref.shape[2]
    else:
        x_ref, xp_ref, xn_ref, cw_ref, cb_ref = refs[:5]
        del refs[:5]
        tl, cb = x_ref.shape[1], x_ref.shape[2]
    w_ref, ba_ref, bx_ref, lam_ref, h0_ref = refs[:5]
    del refs[:5]
    if gated:
        hf_ref, lg_ref = refs[:2]
        del refs[:2]
    o_ref, hl_ref = refs[:2]
    del refs[:2]
    if conv_from == "input+emit":
        xc_out_ref = refs.pop(0)
    a_scr, u_scr, ho_scr, h_scr = refs
    i = pl.program_id(1)
    n = pl.num_programs(1)
    ci = n - 1 - i if reverse else i

    def slot_rows(blk, bb):
        return pl.ds((blk % LRU_GROUP) * nb + bb, tl, stride=SUBLANES)

    @pl.when(i == 0)
    def _():
        for blk in range(cb // LRU_BLOCK_DIM):
            e = blk % LRU_GROUP
            h_scr[blk // LRU_GROUP, e * nb:(e + 1) * nb, :] = (
                h0_ref[:, blk * LRU_BLOCK_DIM:(blk + 1) * LRU_BLOCK_DIM])

    if conv_from == "given":
        xc = xc_ref[...].reshape(nb * tl, cb)
    else:
        xc = _token_conv(x_ref, xp_ref, xn_ref, cw_ref, cb_ref, ci, n, nb)
        if conv_from == "input+emit":
            xc_out_ref[...] = xc.reshape(nb, tl, cb)

    lam = lam_ref[...]
    neg = -lam
    sp = jnp.maximum(neg, 0.0) + jnp.log1p(jnp.exp(-jnp.abs(neg)))
    half_c = (-0.5 * LRU_C * LOG2_E) * sp
    for blk in range(cb // LRU_BLOCK_DIM):
        sl = slice(blk * LRU_BLOCK_DIM, (blk + 1) * LRU_BLOCK_DIM)
        xb = xc[:, sl]
        z = _dot(xb.astype(BF16), w_ref[blk])
        tr = jnp.tanh(z[:, :LRU_BLOCK_DIM] + ba_ref[:, sl])
        ti = jnp.tanh(z[:, LRU_BLOCK_DIM:] + bx_ref[:, sl])
        hc = half_c[:, sl]
        a = jnp.exp2(hc + hc * tr)
        hx = 0.5 * xb
        y = 1.0 - a * a
        root = y * lax.rsqrt(jnp.maximum(y, LRU_TINY))
        u = root * (hx + hx * ti)
        for bb in range(nb):
            a_scr[blk // LRU_GROUP, slot_rows(blk, bb), :] = a[bb * tl:(bb + 1) * tl]
            u_scr[blk // LRU_GROUP, slot_rows(blk, bb), :] = u[bb * tl:(bb + 1) * tl]

    ngrp = cb // LRU_BLOCK_DIM // LRU_GROUP

    def step(t, hs):
        t1 = tl - 1 - 2 * t if reverse else 2 * t
        t2 = t1 - 1 if reverse else t1 + 1
        rows1 = pl.ds(pl.multiple_of(t1 * SUBLANES, SUBLANES), SUBLANES)
        rows2 = pl.ds(pl.multiple_of(t2 * SUBLANES, SUBLANES), SUBLANES)
        out = []
        for g in range(ngrp):
            a1 = a_scr[g, rows1, :]
            u1 = u_scr[g, rows1, :]
            a2 = a_scr[g, rows2, :]
            u2 = u_scr[g, rows2, :]
            h2 = (a2 * a1) * hs[g] + (a2 * u1 + u2)
            ho_scr[g, rows1, :] = a1 * hs[g] + u1
            ho_scr[g, rows2, :] = h2
            out.append(h2)
        return tuple(out)

    hs = lax.fori_loop(0, tl // 2, step, tuple(h_scr[g] for g in range(ngrp)), unroll=4)
    for g in range(ngrp):
        h_scr[g] = hs[g]
    for blk in range(cb // LRU_BLOCK_DIM):
        e = blk % LRU_GROUP
        hl_ref[:, blk * LRU_BLOCK_DIM:(blk + 1) * LRU_BLOCK_DIM] = (
            hs[blk // LRU_GROUP][e * nb:(e + 1) * nb])

    for bb in range(nb):
        hb = jnp.concatenate(
            [ho_scr[blk // LRU_GROUP, slot_rows(blk, bb), :]
             for blk in range(cb // LRU_BLOCK_DIM)], axis=-1)
        if gated:
            hb = (hb + hf_ref[bb].astype(F32)) * _gelu_tanh(lg_ref[bb].astype(F32))
        o_ref[bb] = hb.astype(BF16)


def _lru(p, x_col0, conv_w, conv_b, w_blk, ba, bx, lam, h0, *, reverse, emit_conv=False,
         conv=None, hf=None, lg_col0=None):
    nb, t, _ = p.shape
    tl = min(LRU_TL, t)
    n = t // tl
    cb = LRU_CB
    xb0 = x_col0 // cb
    gated = hf is not None
    hpc = tl // HALO
    assert nb * LRU_GROUP == SUBLANES
    ngrp = cb // LRU_BLOCK_DIM // LRU_GROUP
    conv_from = "given" if conv is not None else ("input+emit" if emit_conv else "input")

    def pos(i):
        return n - 1 - i if reverse else i

    chunk_spec = lambda col0: pl.BlockSpec((nb, tl, cb), lambda c, i: (0, pos(i), col0 + c))
    row_spec = lambda rows: pl.BlockSpec((rows, cb), lambda c, i: (0, c))
    if conv is not None:
        in_specs = [chunk_spec(0)]
        args = [conv]
    else:
        in_specs = [
            chunk_spec(xb0),
            pl.BlockSpec((nb, HALO, cb),
                         lambda c, i: (0, jnp.maximum(pos(i) * hpc - 1, 0), xb0 + c)),
            pl.BlockSpec((nb, HALO, cb),
                         lambda c, i: (0, jnp.minimum((pos(i) + 1) * hpc, t // HALO - 1), xb0 + c)),
            row_spec(4),
            row_spec(1),
        ]
        args = [p, p, p, conv_w, conv_b.reshape(1, -1)]
    in_specs += [
        pl.BlockSpec((cb // LRU_BLOCK_DIM, LRU_BLOCK_DIM, 2 * LRU_BLOCK_DIM), lambda c, i: (c, 0, 0)),
        row_spec(1), row_spec(1), row_spec(1), row_spec(nb),
    ]
    args += [w_blk, ba.reshape(1, -1), bx.reshape(1, -1), lam.reshape(1, -1), h0]
    if gated:
        in_specs += [chunk_spec(0), chunk_spec(lg_col0 // cb)]
        args += [hf, p]
    out_specs = [chunk_spec(0), row_spec(nb)]
    out_shape = [jax.ShapeDtypeStruct((nb, t, LRU_W), BF16), jax.ShapeDtypeStruct((nb, LRU_W), F32)]
    if emit_conv:
        out_specs.append(chunk_spec(0))
        out_shape.append(jax.ShapeDtypeStruct((nb, t, LRU_W), F32))
    kern = functools.partial(_lru_kernel, reverse=reverse, gated=gated, nb=nb, conv_from=conv_from)
    return pl.pallas_call(
        kern,
        grid=(LRU_W // cb, n),
        in_specs=in_specs,
        out_specs=out_specs,
        out_shape=out_shape,
        scratch_shapes=[
            pltpu.VMEM((ngrp, tl * SUBLANES, LRU_BLOCK_DIM), F32),
            pltpu.VMEM((ngrp, tl * SUBLANES, LRU_BLOCK_DIM), F32),
            pltpu.VMEM((ngrp, tl * SUBLANES, LRU_BLOCK_DIM), F32),
            pltpu.VMEM((ngrp, SUBLANES, LRU_BLOCK_DIM), F32),
        ],
        compiler_params=_params("parallel", "arbitrary"),
        name="lru_bwd" if reverse else "lru_fwd",
    )(*args)


MERGE_TM = 512
MERGE_TK = 1024


def _merge_kernel(mod_ref, attn_ref, lru_ref, ga_ref, gl_ref, w_ref, o_ref, acc_scr):
    kk = pl.program_id(2)

    @pl.when(kk == 0)
    def _():
        acc_scr[...] = jnp.zeros_like(acc_scr)

    m = (attn_ref[0] * (1.0 + jnp.tanh(ga_ref[0])) + lru_ref[0] * (1.0 + jnp.tanh(gl_ref[0])))
    w_rows = pl.ds(pl.multiple_of(kk * MERGE_TK, MERGE_TK), MERGE_TK)
    acc_scr[...] += _dot(m, w_ref[w_rows, :])

    @pl.when(kk == pl.num_programs(2) - 1)
    def _():
        gate = mod_ref[0, 5:6, :]

        def body(rows):
            o_ref[0, rows, :] = (gate * acc_scr[rows, :]).astype(BF16)

        _for_row_chunks(acc_scr.shape[0], body)


def _merge(mod, attn, lru, p, w_out):
    b, s, d = attn.shape
    ga0 = OFF_GA // MERGE_TK
    gl0 = OFF_GL // MERGE_TK
    return pl.pallas_call(
        _merge_kernel,
        grid=(b, s // MERGE_TM, d // MERGE_TK),
        in_specs=[
            pl.BlockSpec((1, N_MOD, d), lambda bb, i, k: (bb, 0, 0)),
            pl.BlockSpec((1, MERGE_TM, MERGE_TK), lambda bb, i, k: (bb, i, k)),
            pl.BlockSpec((1, MERGE_TM, MERGE_TK), lambda bb, i, k: (bb, i, k)),
            pl.BlockSpec((1, MERGE_TM, MERGE_TK), lambda bb, i, k: (bb, i, ga0 + k)),
            pl.BlockSpec((1, MERGE_TM, MERGE_TK), lambda bb, i, k: (bb, i, gl0 + k)),
            pl.BlockSpec((d, d), lambda bb, i, k: (0, 0)),
        ],
        out_specs=pl.BlockSpec((1, MERGE_TM, d), lambda bb, i, k: (bb, i, 0)),
        out_shape=jax.ShapeDtypeStruct((b, s, d), BF16),
        scratch_shapes=[pltpu.VMEM((MERGE_TM, d), F32)],
        compiler_params=_params("parallel", "parallel", "arbitrary"),
        name="merge",
    )(mod, attn, lru, p, p, w_out)


def _deinterleave_heads(w, n_heads):
    lead = w.shape[:-1]
    w = w.reshape(lead + (n_heads, HEAD_DIM // 2, 2))
    w = jnp.swapaxes(w, -1, -2)
    return w.reshape(lead + (n_heads * HEAD_DIM,))


def _rope_tables(n_tok):
    rows = n_tok // GRID_W
    row = jnp.repeat(jnp.arange(rows, dtype=F32), GRID_W)
    col = jnp.tile(jnp.arange(GRID_W, dtype=F32), rows)
    axis_dims = HEAD_DIM // 2
    freqs = ROPE_THETA ** (-jnp.arange(0, axis_dims, 2, dtype=F32) / axis_dims)
    ang = jnp.concatenate([row[:, None] * freqs, col[:, None] * freqs], axis=-1)
    cos, sin = jnp.cos(ang), jnp.sin(ang)
    return jnp.concatenate([cos, cos], axis=-1), jnp.concatenate([-sin, sin], axis=-1)


def kernel(x, c, ctx, c_ctx, w_mod, b_mod, norm_g, ffn_wg, ffn_wu, ffn_wd, w_in, w_out, q_norm_g,
           k_norm_g, conv_w, conv_b, lru_wa, lru_ba, lru_wx, lru_bx, lru_lambda, final_norm_g):
    b, s, d = x.shape
    n_ctx = ctx.shape[1]
    ctx_row = b

    cc = jnp.concatenate([c, c_ctx[None], jnp.zeros((SUBLANES - b - 1, d), F32)], axis=0)
    mod = _modulation(cc, w_mod[0], b_mod[0]).reshape(SUBLANES, N_MOD, d)

    wg = ffn_wg[0].astype(BF16)
    wu = ffn_wu[0].astype(BF16)
    wd = ffn_wd[0].astype(BF16)
    w_in_p = _prep_w_in(w_in[0])
    w_out_b = (0.5 * w_out[0]).astype(BF16)
    qkg = jnp.stack([
        _deinterleave_heads(q_norm_g[0], 1) * (HEAD_DIM ** -0.5 * LOG2_E),
        _deinterleave_heads(k_norm_g[0], 1),
    ])
    cosf, sinf = _rope_tables(s)
    w_lru = (0.5 * jnp.concatenate([lru_wa[0], lru_wx[0]], axis=-1)).astype(BF16)
    ba_h = 0.5 * lru_ba[0]
    bx_h = 0.5 * lru_bx[0]

    lat_row = lambda bb: bb
    ctx_rowf = lambda bb: ctx_row

    x1 = _ffn(x, mod, lat_row, norm_g[0, 0], wg, wu, wd, final_norm_g,
              ffn_idx=0, mod_base=0, final_norm=False, tm=512)
    ctx_flat = ctx.reshape(1, b * n_ctx, d)
    ctx1 = _ffn(ctx_flat, mod, ctx_rowf, norm_g[0, 0], wg, wu, wd, final_norm_g,
                ffn_idx=0, mod_base=0, final_norm=False, tm=512)

    pc = _proj(ctx1, mod, ctx_rowf, norm_g[0, 1], w_in_p, qkg, cosf, sinf,
               col0=PROJ_TILE_KV, ncol=3, rope=False, tm=b * n_ctx)
    pc = pc.reshape(b, n_ctx, 3 * PROJ_TN)
    zeros = jnp.zeros((b, LRU_W), F32)
    lx_c = OFF_LX - OFF_K
    lru_f = (w_lru[0], ba_h[0], bx_h[0], lru_lambda[0, 0])
    lru_b = (w_lru[1], ba_h[1], bx_h[1], lru_lambda[0, 1])
    _, hf0 = _lru(pc, lx_c, conv_w[0], conv_b[0], *lru_f, zeros, reverse=False)
    _, hb0 = _lru(pc, lx_c, conv_w[0], conv_b[0], *lru_b, zeros, reverse=True)

    p = _proj(x1, mod, lat_row, norm_g[0, 1], w_in_p, qkg, cosf, sinf,
              col0=0, ncol=IN_W // PROJ_TN, rope=True, tm=1024)
    attn = _attention(p, pc)
    hf, _, xc = _lru(p, OFF_LX, conv_w[0], conv_b[0], *lru_f, hf0, reverse=False, emit_conv=True)
    lru, _ = _lru(p, OFF_LX, conv_w[0], conv_b[0], *lru_b, hb0, reverse=True, conv=xc,
                  hf=hf, lg_col0=OFF_LG)
    mix = _merge(mod, attn, lru, p, w_out_b)

    return _ffn(x1, mod, lat_row, norm_g[0, 2], wg, wu, wd, final_norm_g,
                ffn_idx=1, mod_base=6, final_norm=True, tm=512, delta=mix)
```

```python
import functools

import jax
import jax.numpy as jnp
from jax import lax
from jax.experimental import pallas as pl
from jax.experimental.pallas import tpu as pltpu

D_MODEL = 2048
GRID_W = 64
HEAD_DIM = 128
N_Q_HEADS = 16
N_KV_HEADS = 4
GQA_GROUP = N_Q_HEADS // N_KV_HEADS
ATTN_W = N_Q_HEADS * HEAD_DIM
KV_W = N_KV_HEADS * HEAD_DIM
LRU_W = D_MODEL
LRU_BLOCK_DIM = 128
LRU_C = 8.0
D_FF = 5632
ROPE_THETA = 10000.0
EPS = 1e-6
N_MOD = 9
FFN_RES = 0.5
LOG2_E = 1.4426950408889634
OFF_Q = 0
OFF_K = OFF_Q + ATTN_W
OFF_V = OFF_K + KV_W
OFF_LX = OFF_V + KV_W
OFF_LG = OFF_LX + LRU_W
OFF_GA = OFF_LG + LRU_W
OFF_GL = OFF_GA + D_MODEL
IN_W = OFF_GL + D_MODEL

SUBLANES = 8
LANES = 128
VMEM_LIMIT = 56 * 1024 * 1024

BF16 = jnp.bfloat16
F32 = jnp.float32


def _params(*sem):
    return pltpu.CompilerParams(dimension_semantics=sem, vmem_limit_bytes=VMEM_LIMIT)


def _dot(a, b):
    return jnp.dot(a, b, preferred_element_type=F32)


def _rms(x):
    return x * lax.rsqrt(jnp.mean(x * x, axis=-1, keepdims=True) + EPS)


ROW_CHUNK = 16


def _for_row_chunks(n_rows, body):
    def step(r, carry):
        body(pl.ds(pl.multiple_of(r * ROW_CHUNK, ROW_CHUNK), ROW_CHUNK))
        return carry
    lax.fori_loop(0, n_rows // ROW_CHUNK, step, 0, unroll=8)


def _norm_modulate(load_rows, g_ref, mod_ref, mod_base, h_scr):
    shift = mod_ref[0, mod_base:mod_base + 1, :]
    gain = g_ref[...] * (1.0 + mod_ref[0, mod_base + 1:mod_base + 2, :])

    def body(rows):
        x = load_rows(rows)
        rs = lax.rsqrt(jnp.mean(x * x, axis=-1, keepdims=True) + EPS)
        h_scr[rows, :] = (x * rs * gain + shift).astype(BF16)

    _for_row_chunks(h_scr.shape[0], body)


MOD_TN = 1024


def _mod_kernel(c_ref, w_ref, b_ref, o_ref):
    def split(x):
        hi = x.astype(BF16)
        return hi, (x - hi.astype(F32)).astype(BF16)

    rows = c_ref.shape[0]
    c = c_ref[...]
    s_hi, s_lo = split(c * jax.nn.sigmoid(c))
    w_hi, w_lo = split(w_ref[...])
    both = _dot(jnp.concatenate([s_hi, s_lo], axis=0), w_hi)
    o_ref[...] = both[:rows] + both[rows:] + _dot(s_hi, w_lo) + b_ref[...]


def _modulation(cc, w_mod, b_mod):
    rows, d = cc.shape
    n = w_mod.shape[1]
    return pl.pallas_call(
        _mod_kernel,
        grid=(n // MOD_TN,),
        in_specs=[
            pl.BlockSpec((rows, d), lambda j: (0, 0)),
            pl.BlockSpec((d, MOD_TN), lambda j: (0, j)),
            pl.BlockSpec((1, MOD_TN), lambda j: (0, j)),
        ],
        out_specs=pl.BlockSpec((rows, MOD_TN), lambda j: (0, j)),
        out_shape=jax.ShapeDtypeStruct((rows, n), F32),
        compiler_params=_params("arbitrary"),
        name="modulation",
    )(cc, w_mod, b_mod.reshape(1, n))


FFN_TF = 512


def _ffn_kernel(*refs, mod_base, final_norm, has_delta):
    if has_delta:
        x_ref, d_ref, *refs = refs
    else:
        x_ref, *refs = refs
    mod_ref, g_ref, wg_ref, wu_ref, wd_ref, fg_ref, o_ref, h_scr, acc_scr = refs
    j = pl.program_id(2)

    def load_rows(rows):
        x = x_ref[0, rows, :]
        return x + d_ref[0, rows, :].astype(F32) if has_delta else x

    @pl.when(j == 0)
    def _():
        _norm_modulate(load_rows, g_ref, mod_ref, mod_base, h_scr)

    h = h_scr[...]
    gate = _dot(h, wg_ref[...])
    up = _dot(h, wu_ref[...])
    act = (gate * jax.nn.sigmoid(gate) * up).astype(BF16)
    acc_scr[...] = jnp.where(j == 0, 0.0, acc_scr[...]) + _dot(act, wd_ref[...])

    @pl.when(j == pl.num_programs(2) - 1)
    def _():
        res_gate = FFN_RES * mod_ref[0, mod_base + 2:mod_base + 3, :]

        def body(rows):
            xo = load_rows(rows) + res_gate * acc_scr[rows, :]
            if final_norm:
                xo = _rms(xo) * fg_ref[...]
            o_ref[0, rows, :] = xo

        _for_row_chunks(acc_scr.shape[0], body)


def _ffn(x, mod, mod_row, norm_g, wg, wu, wd, final_g, *, ffn_idx, mod_base, final_norm, tm,
         delta=None):
    b, s, d = x.shape
    kern = functools.partial(_ffn_kernel, mod_base=mod_base, final_norm=final_norm,
                             has_delta=delta is not None)
    tile_spec = pl.BlockSpec((1, tm, d), lambda bb, i, j: (bb, i, 0))
    acts = [x] if delta is None else [x, delta]
    return pl.pallas_call(
        kern,
        grid=(b, s // tm, D_FF // FFN_TF),
        in_specs=[tile_spec] * len(acts) + [
            pl.BlockSpec((1, N_MOD, d), lambda bb, i, j: (mod_row(bb), 0, 0)),
            pl.BlockSpec((1, d), lambda bb, i, j: (0, 0)),
            pl.BlockSpec((None, d, FFN_TF), lambda bb, i, j: (ffn_idx, 0, j)),
            pl.BlockSpec((None, d, FFN_TF), lambda bb, i, j: (ffn_idx, 0, j)),
            pl.BlockSpec((None, FFN_TF, d), lambda bb, i, j: (ffn_idx, j, 0)),
            pl.BlockSpec((1, d), lambda bb, i, j: (0, 0)),
        ],
        out_specs=pl.BlockSpec((1, tm, d), lambda bb, i, j: (bb, i, 0)),
        out_shape=jax.ShapeDtypeStruct((b, s, d), F32),
        scratch_shapes=[pltpu.VMEM((tm, d), BF16), pltpu.VMEM((tm, d), F32)],
        compiler_params=_params("parallel", "parallel", "arbitrary"),
        name="ffn",
    )(*acts, mod, norm_g.reshape(1, d), wg, wu, wd, final_g.reshape(1, d))


PROJ_TN = 1024
PROJ_TILE_KV = OFF_K // PROJ_TN


def _proj_kernel(x_ref, mod_ref, g_ref, w_ref, qkg_ref, cos_ref, sin_ref, o_ref, h_scr,
                 *, mod_base, col0, rope):
    jj = pl.program_id(2)
    j = jj + col0

    @pl.when(jj == 0)
    def _():
        _norm_modulate(lambda rows: x_ref[0, rows, :], g_ref, mod_ref, mod_base, h_scr)

    z = _dot(h_scr[...], w_ref[...])

    def head_norm(t, gain):
        sq = t * t
        hi = sq.astype(BF16)
        lo = (sq - hi.astype(F32)).astype(BF16)
        mean_mat = jnp.full((2 * HEAD_DIM, HEAD_DIM), 1.0 / HEAD_DIM, BF16)
        ms = _dot(jnp.concatenate([hi, lo], axis=1), mean_mat)
        y = t * lax.rsqrt(ms + EPS) * gain
        if rope:
            y = y * cos_ref[...] + pltpu.roll(y, HEAD_DIM // 2, axis=1) * sin_ref[...]
        return y.astype(BF16)

    def store_heads(n_heads, gain):
        for hh in range(n_heads):
            sl = slice(hh * HEAD_DIM, (hh + 1) * HEAD_DIM)
            o_ref[0, :, sl] = head_norm(z[:, sl], gain)

    @pl.when(j < PROJ_TILE_KV)
    def _():
        store_heads(PROJ_TN // HEAD_DIM, qkg_ref[0:1, :])

    @pl.when(j == PROJ_TILE_KV)
    def _():
        store_heads(N_KV_HEADS, qkg_ref[1:2, :])
        o_ref[0, :, KV_W:] = z[:, KV_W:].astype(BF16)

    @pl.when(j > PROJ_TILE_KV)
    def _():
        o_ref[0] = z.astype(BF16)


def _proj(x, mod, mod_row, norm_g, w_in, qkg, cosf, sinf, *, col0, ncol, rope, tm):
    b, s, d = x.shape
    kern = functools.partial(_proj_kernel, mod_base=3, col0=col0, rope=rope)
    return pl.pallas_call(
        kern,
        grid=(b, s // tm, ncol),
        in_specs=[
            pl.BlockSpec((1, tm, d), lambda bb, i, j: (bb, i, 0)),
            pl.BlockSpec((1, N_MOD, d), lambda bb, i, j: (mod_row(bb), 0, 0)),
            pl.BlockSpec((1, d), lambda bb, i, j: (0, 0)),
            pl.BlockSpec((d, PROJ_TN), lambda bb, i, j: (0, j + col0)),
            pl.BlockSpec((2, HEAD_DIM), lambda bb, i, j: (0, 0)),
            pl.BlockSpec((tm, HEAD_DIM), lambda bb, i, j: (i, 0)),
            pl.BlockSpec((tm, HEAD_DIM), lambda bb, i, j: (i, 0)),
        ],
        out_specs=pl.BlockSpec((1, tm, PROJ_TN), lambda bb, i, j: (bb, i, j)),
        out_shape=jax.ShapeDtypeStruct((b, s, ncol * PROJ_TN), BF16),
        scratch_shapes=[pltpu.VMEM((tm, d), BF16)],
        compiler_params=_params("parallel", "parallel", "arbitrary"),
        name="proj",
    )(x, mod, norm_g.reshape(1, d), w_in, qkg, cosf, sinf)


PREP_TR = 1024
PROJ_TILE_GA = OFF_GA // PROJ_TN


def _prep_w_in_kernel(w_ref, o_ref):
    j = pl.program_id(1)
    half = HEAD_DIM // 2
    src = lax.broadcasted_iota(jnp.int32, (HEAD_DIM, HEAD_DIM), 0)
    dst = lax.broadcasted_iota(jnp.int32, (HEAD_DIM, HEAD_DIM), 1)
    perm = (src == jnp.where(dst < half, 2 * dst, 2 * (dst - half) + 1)).astype(BF16)

    def reorder_heads(n_heads):
        for hh in range(n_heads):
            sl = slice(hh * HEAD_DIM, (hh + 1) * HEAD_DIM)
            o_ref[:, sl] = _dot(w_ref[:, sl].astype(BF16), perm).astype(BF16)

    @pl.when(j < PROJ_TILE_KV)
    def _():
        reorder_heads(PROJ_TN // HEAD_DIM)

    @pl.when(j == PROJ_TILE_KV)
    def _():
        reorder_heads(N_KV_HEADS)
        o_ref[:, KV_W:] = w_ref[:, KV_W:].astype(BF16)

    @pl.when((j > PROJ_TILE_KV) & (j < PROJ_TILE_GA))
    def _():
        o_ref[...] = w_ref[...].astype(BF16)

    @pl.when(j >= PROJ_TILE_GA)
    def _():
        o_ref[...] = (0.5 * w_ref[...]).astype(BF16)


def _prep_w_in(w_in):
    d, n = w_in.shape
    return pl.pallas_call(
        _prep_w_in_kernel,
        grid=(d // PREP_TR, n // PROJ_TN),
        in_specs=[pl.BlockSpec((PREP_TR, PROJ_TN), lambda i, j: (i, j))],
        out_specs=pl.BlockSpec((PREP_TR, PROJ_TN), lambda i, j: (i, j)),
        out_shape=jax.ShapeDtypeStruct((d, n), BF16),
        compiler_params=_params("parallel", "parallel"),
        name="prep_w_in",
    )(w_in)


ATTN_TQ = 1024
ATTN_SUB = 256
ATTN_KEY_CHUNK = 256


def _attn_kernel(q_ref, qn_ref, k_ref, v_ref, kc_ref, vc_ref, o_ref,
                 vt_scr, vct_scr, s_scr, m_scr):
    i = pl.program_id(2)
    s_len = k_ref.shape[1]
    c_len = kc_ref.shape[1]
    nt = (((1,), (1,)), ((), ()))
    pair_w = 2 * HEAD_DIM
    chunks = [(k_ref, vt_scr, r, ATTN_KEY_CHUNK, r) for r in range(0, s_len, ATTN_KEY_CHUNK)]
    chunks.append((kc_ref, vct_scr, 0, c_len, s_len))
    units = [(r0, pr * pair_w) for pr in range(GQA_GROUP // 2)
             for r0 in range(0, q_ref.shape[1], ATTN_SUB)]

    def stack_pair(ref, row0, col0):
        rows = slice(row0, row0 + ATTN_SUB)
        return jnp.concatenate([ref[0, rows, col0:col0 + HEAD_DIM],
                                ref[0, rows, col0 + HEAD_DIM:col0 + pair_w]], axis=0)

    def fold8(x, op):
        return op(x.reshape(x.shape[0] // SUBLANES, SUBLANES, x.shape[1]), axis=0)

    def chunk_scores(q2, slot, chunk, mpart):
        kref, _, r, n, row = chunk
        s = lax.dot_general(kref[0, r:r + n, :], q2, nt, preferred_element_type=F32)
        s_scr[slot, row:row + n, :] = s
        cm = fold8(s, jnp.max)
        return cm if mpart is None else jnp.maximum(mpart, cm)

    def phase(q_next, slot_next, slot_cur, unit):
        row0, col0 = unit
        m_cur = m_scr[slot_cur]
        mpart = lpart = ot = None
        for chunk in chunks:
            _, vtref, r, n, row = chunk
            mpart = chunk_scores(q_next, slot_next, chunk, mpart)
            p = jnp.exp2(s_scr[slot_cur, row:row + n, :] - m_cur)
            cl = fold8(p, jnp.sum)
            lpart = cl if lpart is None else lpart + cl
            pv = _dot(vtref[:, r:r + n], p.astype(BF16))
            ot = pv if ot is None else ot + pv
        m_scr[slot_next] = jnp.max(mpart, axis=0, keepdims=True)
        ot = ot * (1.0 / jnp.sum(lpart, axis=0, keepdims=True))
        for e in range(2):
            o_ref[0, row0:row0 + ATTN_SUB, col0 + e * HEAD_DIM:col0 + (e + 1) * HEAD_DIM] = (
                ot[:, e * ATTN_SUB:(e + 1) * ATTN_SUB].T.astype(BF16))

    @pl.when(i == 0)
    def _():
        vt_scr[...] = v_ref[0].T
        vct_scr[...] = vc_ref[0].T
        q2 = stack_pair(q_ref, *units[0])
        mpart = None
        for chunk in chunks:
            mpart = chunk_scores(q2, 0, chunk, mpart)
        m_scr[0] = jnp.max(mpart, axis=0, keepdims=True)

    for u, unit in enumerate(units):
        if u + 1 < len(units):
            q_next = stack_pair(q_ref, *units[u + 1])
        else:
            q_next = stack_pair(qn_ref, 0, 0)
        phase(q_next, (u + 1) % 2, u % 2, unit)


def _attention(p, pc):
    b, s, _ = p.shape
    c = pc.shape[1]
    gw = GQA_GROUP * HEAD_DIM
    kb = OFF_K // HEAD_DIM
    vb = OFF_V // HEAD_DIM
    n = s // ATTN_TQ
    return pl.pallas_call(
        _attn_kernel,
        grid=(b, N_KV_HEADS, n),
        in_specs=[
            pl.BlockSpec((1, ATTN_TQ, gw), lambda bb, g, i: (bb, i, g)),
            pl.BlockSpec((1, ATTN_SUB, 2 * HEAD_DIM),
                         lambda bb, g, i: (bb, jnp.minimum(i + 1, n - 1) * (ATTN_TQ // ATTN_SUB),
                                           2 * g)),
            pl.BlockSpec((1, s, HEAD_DIM), lambda bb, g, i: (bb, 0, kb + g)),
            pl.BlockSpec((1, s, HEAD_DIM), lambda bb, g, i: (bb, 0, vb + g)),
            pl.BlockSpec((1, c, HEAD_DIM), lambda bb, g, i: (bb, 0, g)),
            pl.BlockSpec((1, c, HEAD_DIM), lambda bb, g, i: (bb, 0, N_KV_HEADS + g)),
        ],
        out_specs=pl.BlockSpec((1, ATTN_TQ, gw), lambda bb, g, i: (bb, i, g)),
        out_shape=jax.ShapeDtypeStruct((b, s, ATTN_W), BF16),
        scratch_shapes=[
            pltpu.VMEM((HEAD_DIM, s), BF16),
            pltpu.VMEM((HEAD_DIM, c), BF16),
            pltpu.VMEM((2, s + c, 2 * ATTN_SUB), F32),
            pltpu.VMEM((2, 1, 2 * ATTN_SUB), F32),
        ],
        compiler_params=_params("arbitrary", "arbitrary", "arbitrary"),
        name="attention",
    )(p, p, p, p, pc, pc)


LRU_CB = 512
LRU_TL = 256
LRU_TINY = 1e-30
LRU_GROUP = 2
HALO = SUBLANES
CONV_LEFT = 2


def _gelu_tanh(x):
    return 0.5 * x * (1.0 + jnp.tanh(0.7978845608028654 * (x + 0.044715 * x * x * x)))


def _token_conv(x_ref, xp_ref, xn_ref, cw_ref, cb_ref, ci, n, nb):
    tl, cb = x_ref.shape[1], x_ref.shape[2]
    ri = lax.broadcasted_iota(jnp.int32, (tl, tl), 0)
    cj = lax.broadcasted_iota(jnp.int32, (tl, tl), 1)
    shifts = jnp.concatenate(
        [(cj == ri + d).astype(BF16) for d in (-2, -1, 1)], axis=0)
    row8 = lax.broadcasted_iota(jnp.int32, (HALO, cb), 0)
    has_prev = (ci > 0).astype(F32)
    has_next = (ci < n - 1).astype(F32)
    w0, w1, w2, w3 = (cw_ref[k:k + 1, :] for k in range(4))
    xcs = []
    for bb in range(nb):
        xb = x_ref[bb]
        sh = _dot(shifts, xb)
        y = (cb_ref[...] + w2 * xb.astype(F32) + w0 * sh[:tl] + w1 * sh[tl:2 * tl]
             + w3 * sh[2 * tl:])
        prev = xp_ref[bb].astype(F32) * has_prev
        nxt = xn_ref[bb].astype(F32) * has_next
        head = (w0 * jnp.where(row8 < 2, pltpu.roll(prev, 2, axis=0), 0.0)
                + w1 * jnp.where(row8 < 1, pltpu.roll(prev, 1, axis=0), 0.0))
        tail = w3 * jnp.where(row8 == HALO - 1, pltpu.roll(nxt, HALO - 1, axis=0), 0.0)
        xcs.append(jnp.concatenate(
            [y[:HALO] + head, y[HALO:tl - HALO], y[tl - HALO:] + tail], axis=0))
    return jnp.concatenate(xcs, axis=0)


def _lru_kernel(*refs, reverse, add_other, nb, conv_from):
    refs = list(refs)
    if conv_from == "given":
        xc_ref = refs.pop(0)
        tl, cb = xc_ref.shape[1], xc_ref.shape[2]
    else:
        x_ref, xp_ref, xn_ref, cw_ref, cb_ref = refs[:5]
        del refs[:5]
        tl, cb = x_ref.shape[1], x_ref.shape[2]
    w_ref, ba_ref, bx_ref, lam_ref, h0_ref = refs[:5]
    del refs[:5]
    if add_other:
        hf_ref = refs.pop(0)
    o_ref, hl_ref = refs[:2]
    del refs[:2]
    if conv_from == "input+emit":
        xc_out_ref = refs.pop(0)
    a_scr, u_scr, ho_scr, h_scr = refs
    i = pl.program_id(1)
    n = pl.num_programs(1)
    ci = n - 1 - i if reverse else i

    def slot_rows(blk, bb):
        return pl.ds((blk % LRU_GROUP) * nb + bb, tl, stride=SUBLANES)

    @pl.when(i == 0)
    def _():
        for blk in range(cb // LRU_BLOCK_DIM):
            e = blk % LRU_GROUP
            h_scr[blk // LRU_GROUP, e * nb:(e + 1) * nb, :] = (
                h0_ref[:, blk * LRU_BLOCK_DIM:(blk + 1) * LRU_BLOCK_DIM])

    if conv_from == "given":
        xc = xc_ref[...].reshape(nb * tl, cb)
    else:
        xc = _token_conv(x_ref, xp_ref, xn_ref, cw_ref, cb_ref, ci, n, nb)
        if conv_from == "input+emit":
            xc_out_ref[...] = xc.reshape(nb, tl, cb)

    lam = lam_ref[...]
    neg = -lam
    sp = jnp.maximum(neg, 0.0) + jnp.log1p(jnp.exp(-jnp.abs(neg)))
    half_c = (-0.5 * LRU_C * LOG2_E) * sp
    for blk in range(cb // LRU_BLOCK_DIM):
        sl = slice(blk * LRU_BLOCK_DIM, (blk + 1) * LRU_BLOCK_DIM)
        xb = xc[:, sl]
        z = _dot(xb.astype(BF16), w_ref[blk])
        tr = jnp.tanh(z[:, :LRU_BLOCK_DIM] + ba_ref[:, sl])
        ti = jnp.tanh(z[:, LRU_BLOCK_DIM:] + bx_ref[:, sl])
        hc = half_c[:, sl]
        a = jnp.exp2(hc + hc * tr)
        hx = 0.5 * xb
        y = 1.0 - a * a
        root = y * lax.rsqrt(jnp.maximum(y, LRU_TINY))
        u = root * (hx + hx * ti)
        for bb in range(nb):
            a_scr[blk // LRU_GROUP, slot_rows(blk, bb), :] = a[bb * tl:(bb + 1) * tl]
            u_scr[blk // LRU_GROUP, slot_rows(blk, bb), :] = u[bb * tl:(bb + 1) * tl]

    ngrp = cb // LRU_BLOCK_DIM // LRU_GROUP

    def step(t, hs):
        pair = tl // 2 - 1 - t if reverse else t
        base = pl.multiple_of(pair * (2 * SUBLANES), 2 * SUBLANES)
        first, second = (SUBLANES, 0) if reverse else (0, SUBLANES)
        rows1 = pl.ds(base + first, SUBLANES)
        rows2 = pl.ds(base + second, SUBLANES)
        out = []
        for g in range(ngrp):
            a1 = a_scr[g, rows1, :]
            u1 = u_scr[g, rows1, :]
            a2 = a_scr[g, rows2, :]
            u2 = u_scr[g, rows2, :]
            h2 = (a2 * a1) * hs[g] + (a2 * u1 + u2)
            ho_scr[g, rows1, :] = a1 * hs[g] + u1
            ho_scr[g, rows2, :] = h2
            out.append(h2)
        return tuple(out)

    hs = lax.fori_loop(0, tl // 2, step, tuple(h_scr[g] for g in range(ngrp)), unroll=4)
    for g in range(ngrp):
        h_scr[g] = hs[g]
    for blk in range(cb // LRU_BLOCK_DIM):
        e = blk % LRU_GROUP
        hl_ref[:, blk * LRU_BLOCK_DIM:(blk + 1) * LRU_BLOCK_DIM] = (
            hs[blk // LRU_GROUP][e * nb:(e + 1) * nb])

    for bb in range(nb):
        hb = jnp.concatenate(
            [ho_scr[blk // LRU_GROUP, slot_rows(blk, bb), :]
             for blk in range(cb // LRU_BLOCK_DIM)], axis=-1)
        if add_other:
            hb = hb + hf_ref[bb].astype(F32)
        o_ref[bb] = hb.astype(BF16)


def _lru(p, x_col0, conv_w, conv_b, w_blk, ba, bx, lam, h0, *, reverse, emit_conv=False,
         conv=None, hf=None):
    nb, t, _ = p.shape
    tl = min(LRU_TL, t)
    n = t // tl
    cb = LRU_CB
    xb0 = x_col0 // cb
    hpc = tl // HALO
    assert nb * LRU_GROUP == SUBLANES
    ngrp = cb // LRU_BLOCK_DIM // LRU_GROUP
    conv_from = "given" if conv is not None else ("input+emit" if emit_conv else "input")

    def pos(i):
        return n - 1 - i if reverse else i

    chunk_spec = lambda col0: pl.BlockSpec((nb, tl, cb), lambda c, i: (0, pos(i), col0 + c))
    row_spec = lambda rows: pl.BlockSpec((rows, cb), lambda c, i: (0, c))
    if conv is not None:
        in_specs = [chunk_spec(0)]
        args = [conv]
    else:
        in_specs = [
            chunk_spec(xb0),
            pl.BlockSpec((nb, HALO, cb),
                         lambda c, i: (0, jnp.maximum(pos(i) * hpc - 1, 0), xb0 + c)),
            pl.BlockSpec((nb, HALO, cb),
                         lambda c, i: (0, jnp.minimum((pos(i) + 1) * hpc, t // HALO - 1), xb0 + c)),
            row_spec(4),
            row_spec(1),
        ]
        args = [p, p, p, conv_w, conv_b.reshape(1, -1)]
    in_specs += [
        pl.BlockSpec((cb // LRU_BLOCK_DIM, LRU_BLOCK_DIM, 2 * LRU_BLOCK_DIM), lambda c, i: (c, 0, 0)),
        row_spec(1), row_spec(1), row_spec(1), row_spec(nb),
    ]
    args += [w_blk, ba.reshape(1, -1), bx.reshape(1, -1), lam.reshape(1, -1), h0]
    if hf is not None:
        in_specs.append(chunk_spec(0))
        args.append(hf)
    out_specs = [chunk_spec(0), row_spec(nb)]
    out_shape = [jax.ShapeDtypeStruct((nb, t, LRU_W), BF16), jax.ShapeDtypeStruct((nb, LRU_W), F32)]
    if emit_conv:
        out_specs.append(chunk_spec(0))
        out_shape.append(jax.ShapeDtypeStruct((nb, t, LRU_W), F32))
    kern = functools.partial(_lru_kernel, reverse=reverse, add_other=hf is not None, nb=nb,
                             conv_from=conv_from)
    return pl.pallas_call(
        kern,
        grid=(LRU_W // cb, n),
        in_specs=in_specs,
        out_specs=out_specs,
        out_shape=out_shape,
        scratch_shapes=[
            pltpu.VMEM((ngrp, tl * SUBLANES, LRU_BLOCK_DIM), F32),
            pltpu.VMEM((ngrp, tl * SUBLANES, LRU_BLOCK_DIM), F32),
            pltpu.VMEM((ngrp, tl * SUBLANES, LRU_BLOCK_DIM), F32),
            pltpu.VMEM((ngrp, SUBLANES, LRU_BLOCK_DIM), F32),
        ],
        compiler_params=_params("parallel", "arbitrary"),
        name="lru_bwd" if reverse else "lru_fwd",
    )(*args)


MERGE_TM = 1024
MERGE_TK = 1024
MERGE_ROWS = 128


def _merge_kernel(mod_ref, attn_ref, hsum_ref, lg_ref, ga_ref, gl_ref, w_ref, o_ref, acc_scr):
    kk = pl.program_id(2)

    w = w_ref[pl.ds(pl.multiple_of(kk * MERGE_TK, MERGE_TK), MERGE_TK), :]
    for r0 in range(0, acc_scr.shape[0], MERGE_ROWS):
        rows = slice(r0, r0 + MERGE_ROWS)
        lru = (hsum_ref[0, rows, :].astype(F32)
               * _gelu_tanh(lg_ref[0, rows, :].astype(F32))).astype(BF16)
        m = (attn_ref[0, rows, :] * (1.0 + jnp.tanh(ga_ref[0, rows, :]))
             + lru * (1.0 + jnp.tanh(gl_ref[0, rows, :])))
        acc_scr[rows, :] = jnp.where(kk == 0, 0.0, acc_scr[rows, :]) + _dot(m, w)

    @pl.when(kk == pl.num_programs(2) - 1)
    def _():
        gate = mod_ref[0, 5:6, :]

        def body(rows):
            o_ref[0, rows, :] = (gate * acc_scr[rows, :]).astype(BF16)

        _for_row_chunks(acc_scr.shape[0], body)


def _merge(mod, attn, hsum, p, w_out):
    b, s, d = attn.shape
    lg0 = OFF_LG // MERGE_TK
    ga0 = OFF_GA // MERGE_TK
    gl0 = OFF_GL // MERGE_TK
    return pl.pallas_call(
        _merge_kernel,
        grid=(b, s // MERGE_TM, d // MERGE_TK),
        in_specs=[
            pl.BlockSpec((1, N_MOD, d), lambda bb, i, k: (bb, 0, 0)),
            pl.BlockSpec((1, MERGE_TM, MERGE_TK), lambda bb, i, k: (bb, i, k)),
            pl.BlockSpec((1, MERGE_TM, MERGE_TK), lambda bb, i, k: (bb, i, k)),
            pl.BlockSpec((1, MERGE_TM, MERGE_TK), lambda bb, i, k: (bb, i, lg0 + k)),
            pl.BlockSpec((1, MERGE_TM, MERGE_TK), lambda bb, i, k: (bb, i, ga0 + k)),
            pl.BlockSpec((1, MERGE_TM, MERGE_TK), lambda bb, i, k: (bb, i, gl0 + k)),
            pl.BlockSpec((d, d), lambda bb, i, k: (0, 0)),
        ],
        out_specs=pl.BlockSpec((1, MERGE_TM, d), lambda bb, i, k: (bb, i, 0)),
        out_shape=jax.ShapeDtypeStruct((b, s, d), BF16),
        scratch_shapes=[pltpu.VMEM((MERGE_TM, d), F32)],
        compiler_params=_params("parallel", "parallel", "arbitrary"),
        name="merge",
    )(mod, attn, hsum, p, p, p, w_out)


def _deinterleave_heads(w, n_heads):
    lead = w.shape[:-1]
    w = w.reshape(lead + (n_heads, HEAD_DIM // 2, 2))
    w = jnp.swapaxes(w, -1, -2)
    return w.reshape(lead + (n_heads * HEAD_DIM,))


def _rope_tables(n_tok):
    rows = n_tok // GRID_W
    row = jnp.repeat(jnp.arange(rows, dtype=F32), GRID_W)
    col = jnp.tile(jnp.arange(GRID_W, dtype=F32), rows)
    axis_dims = HEAD_DIM // 2
    freqs = ROPE_THETA ** (-jnp.arange(0, axis_dims, 2, dtype=F32) / axis_dims)
    ang = jnp.concatenate([row[:, None] * freqs, col[:, None] * freqs], axis=-1)
    cos, sin = jnp.cos(ang), jnp.sin(ang)
    return jnp.concatenate([cos, cos], axis=-1), jnp.concatenate([-sin, sin], axis=-1)


def kernel(x, c, ctx, c_ctx, w_mod, b_mod, norm_g, ffn_wg, ffn_wu, ffn_wd, w_in, w_out, q_norm_g,
           k_norm_g, conv_w, conv_b, lru_wa, lru_ba, lru_wx, lru_bx, lru_lambda, final_norm_g):
    b, s, d = x.shape
    n_ctx = ctx.shape[1]
    ctx_row = b

    cc = jnp.concatenate([c, c_ctx[None], jnp.zeros((SUBLANES - b - 1, d), F32)], axis=0)
    mod = _modulation(cc, w_mod[0], b_mod[0]).reshape(SUBLANES, N_MOD, d)

    wg = ffn_wg[0].astype(BF16)
    wu = ffn_wu[0].astype(BF16)
    wd = ffn_wd[0].astype(BF16)
    w_in_p = _prep_w_in(w_in[0])
    w_out_b = (0.5 * w_out[0]).astype(BF16)
    qkg = jnp.stack([
        _deinterleave_heads(q_norm_g[0], 1) * (HEAD_DIM ** -0.5 * LOG2_E),
        _deinterleave_heads(k_norm_g[0], 1),
    ])
    cosf, sinf = _rope_tables(s)
    w_lru = (0.5 * jnp.concatenate([lru_wa[0], lru_wx[0]], axis=-1)).astype(BF16)
    ba_h = 0.5 * lru_ba[0]
    bx_h = 0.5 * lru_bx[0]

    lat_row = lambda bb: bb
    ctx_rowf = lambda bb: ctx_row

    x1 = _ffn(x, mod, lat_row, norm_g[0, 0], wg, wu, wd, final_norm_g,
              ffn_idx=0, mod_base=0, final_norm=False, tm=512)
    ctx_flat = ctx.reshape(1, b * n_ctx, d)
    ctx1 = _ffn(ctx_flat, mod, ctx_rowf, norm_g[0, 0], wg, wu, wd, final_norm_g,
                ffn_idx=0, mod_base=0, final_norm=False, tm=512)

    pc = _proj(ctx1, mod, ctx_rowf, norm_g[0, 1], w_in_p, qkg, cosf, sinf,
               col0=PROJ_TILE_KV, ncol=3, rope=False, tm=b * n_ctx)
    pc = pc.reshape(b, n_ctx, 3 * PROJ_TN)
    zeros = jnp.zeros((b, LRU_W), F32)
    lx_c = OFF_LX - OFF_K
    lru_f = (w_lru[0], ba_h[0], bx_h[0], lru_lambda[0, 0])
    lru_b = (w_lru[1], ba_h[1], bx_h[1], lru_lambda[0, 1])
    _, hf0 = _lru(pc, lx_c, conv_w[0], conv_b[0], *lru_f, zeros, reverse=False)
    _, hb0 = _lru(pc, lx_c, conv_w[0], conv_b[0], *lru_b, zeros, reverse=True)

    p = _proj(x1, mod, lat_row, norm_g[0, 1], w_in_p, qkg, cosf, sinf,
              col0=0, ncol=IN_W // PROJ_TN, rope=True, tm=1024)
    attn = _attention(p, pc)
    hf, _, xc = _lru(p, OFF_LX, conv_w[0], conv_b[0], *lru_f, hf0, reverse=False, emit_conv=True)
    hsum, _ = _lru(p, OFF_LX, conv_w[0], conv_b[0], *lru_b, hb0, reverse=True, conv=xc, hf=hf)
    mix = _merge(mod, attn, hsum, p, w_out_b)

    return _ffn(x1, mod, lat_row, norm_g[0, 2], wg, wu, wd, final_norm_g,
                ffn_idx=1, mod_base=6, final_norm=True, tm=512, delta=mix)
```

```python
import functools

import jax
import jax.numpy as jnp
from jax import lax
from jax.experimental import pallas as pl
from jax.experimental.pallas import tpu as pltpu

D_MODEL = 2048
GRID_W = 64
HEAD_DIM = 128
N_Q_HEADS = 16
N_KV_HEADS = 4
GQA_GROUP = N_Q_HEADS // N_KV_HEADS
ATTN_W = N_Q_HEADS * HEAD_DIM
KV_W = N_KV_HEADS * HEAD_DIM
LRU_W = D_MODEL
LRU_BLOCK_DIM = 128
LRU_C = 8.0
D_FF = 5632
ROPE_THETA = 10000.0
EPS = 1e-6
N_MOD = 9
FFN_RES = 0.5
LOG2_E = 1.4426950408889634
OFF_Q = 0
OFF_K = OFF_Q + ATTN_W
OFF_V = OFF_K + KV_W
OFF_LX = OFF_V + KV_W
OFF_LG = OFF_LX + LRU_W
OFF_GA = OFF_LG + LRU_W
OFF_GL = OFF_GA + D_MODEL
IN_W = OFF_GL + D_MODEL

SUBLANES = 8
LANES = 128
VMEM_LIMIT = 56 * 1024 * 1024

BF16 = jnp.bfloat16
F32 = jnp.float32


def _params(*sem):
    return pltpu.CompilerParams(dimension_semantics=sem, vmem_limit_bytes=VMEM_LIMIT)


def _dot(a, b):
    return jnp.dot(a, b, preferred_element_type=F32)


def _rms(x):
    return x * lax.rsqrt(jnp.mean(x * x, axis=-1, keepdims=True) + EPS)


ROW_CHUNK = 16


def _for_row_chunks(n_rows, body):
    def step(r, carry):
        body(pl.ds(pl.multiple_of(r * ROW_CHUNK, ROW_CHUNK), ROW_CHUNK))
        return carry
    lax.fori_loop(0, n_rows // ROW_CHUNK, step, 0, unroll=8)


def _norm_modulate(load_rows, g_ref, mod_ref, mod_base, h_scr):
    shift = mod_ref[0, mod_base:mod_base + 1, :]
    gain = g_ref[...] * (1.0 + mod_ref[0, mod_base + 1:mod_base + 2, :])

    def body(rows):
        x = load_rows(rows)
        rs = lax.rsqrt(jnp.mean(x * x, axis=-1, keepdims=True) + EPS)
        h_scr[rows, :] = (x * rs * gain + shift).astype(BF16)

    _for_row_chunks(h_scr.shape[0], body)


MOD_TN = 1024


def _mod_kernel(c_ref, w_ref, b_ref, o_ref):
    def split(x):
        hi = x.astype(BF16)
        return hi, (x - hi.astype(F32)).astype(BF16)

    rows = c_ref.shape[0]
    c = c_ref[...]
    s_hi, s_lo = split(c * jax.nn.sigmoid(c))
    w_hi, w_lo = split(w_ref[...])
    both = _dot(jnp.concatenate([s_hi, s_lo], axis=0), w_hi)
    o_ref[...] = both[:rows] + both[rows:] + _dot(s_hi, w_lo) + b_ref[...]


def _modulation(cc, w_mod, b_mod):
    rows, d = cc.shape
    n = w_mod.shape[1]
    return pl.pallas_call(
        _mod_kernel,
        grid=(n // MOD_TN,),
        in_specs=[
            pl.BlockSpec((rows, d), lambda j: (0, 0)),
            pl.BlockSpec((d, MOD_TN), lambda j: (0, j)),
            pl.BlockSpec((1, MOD_TN), lambda j: (0, j)),
        ],
        out_specs=pl.BlockSpec((rows, MOD_TN), lambda j: (0, j)),
        out_shape=jax.ShapeDtypeStruct((rows, n), F32),
        compiler_params=_params("arbitrary"),
        name="modulation",
    )(cc, w_mod, b_mod.reshape(1, n))


FFN_TF = 512


def _ffn_kernel(*refs, mod_base, final_norm, has_delta):
    if has_delta:
        x_ref, d_ref, *refs = refs
    else:
        x_ref, *refs = refs
    mod_ref, g_ref, wg_ref, wu_ref, wd_ref, fg_ref, o_ref, h_scr, acc_scr = refs
    j = pl.program_id(2)

    def load_rows(rows):
        x = x_ref[0, rows, :]
        return x + d_ref[0, rows, :].astype(F32) if has_delta else x

    @pl.when(j == 0)
    def _():
        _norm_modulate(load_rows, g_ref, mod_ref, mod_base, h_scr)

    h = h_scr[...]
    gate = _dot(h, wg_ref[...])
    up = _dot(h, wu_ref[...])
    act = (gate * jax.nn.sigmoid(gate) * up).astype(BF16)
    acc_scr[...] = jnp.where(j == 0, 0.0, acc_scr[...]) + _dot(act, wd_ref[...])

    @pl.when(j == pl.num_programs(2) - 1)
    def _():
        res_gate = FFN_RES * mod_ref[0, mod_base + 2:mod_base + 3, :]

        def body(rows):
            xo = load_rows(rows) + res_gate * acc_scr[rows, :]
            if final_norm:
                xo = _rms(xo) * fg_ref[...]
            o_ref[0, rows, :] = xo

        _for_row_chunks(acc_scr.shape[0], body)


def _ffn(x, mod, mod_row, norm_g, wg, wu, wd, final_g, *, ffn_idx, mod_base, final_norm, tm,
         delta=None):
    b, s, d = x.shape
    kern = functools.partial(_ffn_kernel, mod_base=mod_base, final_norm=final_norm,
                             has_delta=delta is not None)
    tile_spec = pl.BlockSpec((1, tm, d), lambda bb, i, j: (bb, i, 0))
    acts = [x] if delta is None else [x, delta]
    return pl.pallas_call(
        kern,
        grid=(b, s // tm, D_FF // FFN_TF),
        in_specs=[tile_spec] * len(acts) + [
            pl.BlockSpec((1, N_MOD, d), lambda bb, i, j: (mod_row(bb), 0, 0)),
            pl.BlockSpec((1, d), lambda bb, i, j: (0, 0)),
            pl.BlockSpec((None, d, FFN_TF), lambda bb, i, j: (ffn_idx, 0, j)),
            pl.BlockSpec((None, d, FFN_TF), lambda bb, i, j: (ffn_idx, 0, j)),
            pl.BlockSpec((None, FFN_TF, d), lambda bb, i, j: (ffn_idx, j, 0)),
            pl.BlockSpec((1, d), lambda bb, i, j: (0, 0)),
        ],
        out_specs=pl.BlockSpec((1, tm, d), lambda bb, i, j: (bb, i, 0)),
        out_shape=jax.ShapeDtypeStruct((b, s, d), F32),
        scratch_shapes=[pltpu.VMEM((tm, d), BF16), pltpu.VMEM((tm, d), F32)],
        compiler_params=_params("parallel", "parallel", "arbitrary"),
        name="ffn",
    )(*acts, mod, norm_g.reshape(1, d), wg, wu, wd, final_g.reshape(1, d))


PROJ_TN = 1024
PROJ_TILE_KV = OFF_K // PROJ_TN


def _proj_kernel(x_ref, mod_ref, g_ref, w_ref, qkg_ref, cos_ref, sin_ref, o_ref, h_scr,
                 *, mod_base, col0, rope):
    jj = pl.program_id(2)
    j = jj + col0

    @pl.when(jj == 0)
    def _():
        _norm_modulate(lambda rows: x_ref[0, rows, :], g_ref, mod_ref, mod_base, h_scr)

    z = _dot(h_scr[...], w_ref[...])

    def head_norm(t, gain):
        mean_mat = jnp.full((HEAD_DIM, HEAD_DIM), 1.0 / HEAD_DIM, BF16)
        ms = _dot((t * t).astype(BF16), mean_mat)
        y = t * lax.rsqrt(ms + EPS) * gain
        if rope:
            y = y * cos_ref[...] + pltpu.roll(y, HEAD_DIM // 2, axis=1) * sin_ref[...]
        return y.astype(BF16)

    def store_heads(n_heads, gain):
        for hh in range(n_heads):
            sl = slice(hh * HEAD_DIM, (hh + 1) * HEAD_DIM)
            o_ref[0, :, sl] = head_norm(z[:, sl], gain)

    @pl.when(j < PROJ_TILE_KV)
    def _():
        store_heads(PROJ_TN // HEAD_DIM, qkg_ref[0:1, :])

    @pl.when(j == PROJ_TILE_KV)
    def _():
        store_heads(N_KV_HEADS, qkg_ref[1:2, :])
        o_ref[0, :, KV_W:] = z[:, KV_W:].astype(BF16)

    @pl.when(j > PROJ_TILE_KV)
    def _():
        o_ref[0] = z.astype(BF16)


def _proj(x, mod, mod_row, norm_g, w_in, qkg, cosf, sinf, *, col0, ncol, rope, tm):
    b, s, d = x.shape
    kern = functools.partial(_proj_kernel, mod_base=3, col0=col0, rope=rope)
    return pl.pallas_call(
        kern,
        grid=(b, s // tm, ncol),
        in_specs=[
            pl.BlockSpec((1, tm, d), lambda bb, i, j: (bb, i, 0)),
            pl.BlockSpec((1, N_MOD, d), lambda bb, i, j: (mod_row(bb), 0, 0)),
            pl.BlockSpec((1, d), lambda bb, i, j: (0, 0)),
            pl.BlockSpec((d, PROJ_TN), lambda bb, i, j: (0, j + col0)),
            pl.BlockSpec((2, HEAD_DIM), lambda bb, i, j: (0, 0)),
            pl.BlockSpec((tm, HEAD_DIM), lambda bb, i, j: (i, 0)),
            pl.BlockSpec((tm, HEAD_DIM), lambda bb, i, j: (i, 0)),
        ],
        out_specs=pl.BlockSpec((1, tm, PROJ_TN), lambda bb, i, j: (bb, i, j)),
        out_shape=jax.ShapeDtypeStruct((b, s, ncol * PROJ_TN), BF16),
        scratch_shapes=[pltpu.VMEM((tm, d), BF16)],
        compiler_params=_params("parallel", "parallel", "arbitrary"),
        name="proj",
    )(x, mod, norm_g.reshape(1, d), w_in, qkg, cosf, sinf)


PREP_TR = 1024
PROJ_TILE_GA = OFF_GA // PROJ_TN


def _prep_w_in_kernel(w_ref, o_ref):
    j = pl.program_id(1)
    half = HEAD_DIM // 2
    src = lax.broadcasted_iota(jnp.int32, (HEAD_DIM, HEAD_DIM), 0)
    dst = lax.broadcasted_iota(jnp.int32, (HEAD_DIM, HEAD_DIM), 1)
    perm = (src == jnp.where(dst < half, 2 * dst, 2 * (dst - half) + 1)).astype(BF16)

    def reorder_heads(n_heads):
        for hh in range(n_heads):
            sl = slice(hh * HEAD_DIM, (hh + 1) * HEAD_DIM)
            o_ref[:, sl] = _dot(w_ref[:, sl].astype(BF16), perm).astype(BF16)

    @pl.when(j < PROJ_TILE_KV)
    def _():
        reorder_heads(PROJ_TN // HEAD_DIM)

    @pl.when(j == PROJ_TILE_KV)
    def _():
        reorder_heads(N_KV_HEADS)
        o_ref[:, KV_W:] = w_ref[:, KV_W:].astype(BF16)

    @pl.when((j > PROJ_TILE_KV) & (j < PROJ_TILE_GA))
    def _():
        o_ref[...] = w_ref[...].astype(BF16)

    @pl.when(j >= PROJ_TILE_GA)
    def _():
        o_ref[...] = (0.5 * w_ref[...]).astype(BF16)


def _prep_w_in(w_in):
    d, n = w_in.shape
    return pl.pallas_call(
        _prep_w_in_kernel,
        grid=(d // PREP_TR, n // PROJ_TN),
        in_specs=[pl.BlockSpec((PREP_TR, PROJ_TN), lambda i, j: (i, j))],
        out_specs=pl.BlockSpec((PREP_TR, PROJ_TN), lambda i, j: (i, j)),
        out_shape=jax.ShapeDtypeStruct((d, n), BF16),
        compiler_params=_params("parallel", "parallel"),
        name="prep_w_in",
    )(w_in)


ATTN_TQ = 1024
ATTN_SUB = 256
ATTN_KEY_CHUNK = 256


def _attn_kernel(q_ref, qn_ref, k_ref, v_ref, kc_ref, vc_ref, o_ref,
                 vt_scr, vct_scr, s_scr, m_scr):
    i = pl.program_id(2)
    s_len = k_ref.shape[1]
    c_len = kc_ref.shape[1]
    nt = (((1,), (1,)), ((), ()))
    pair_w = 2 * HEAD_DIM
    chunks = [(k_ref, vt_scr, r, ATTN_KEY_CHUNK, r) for r in range(0, s_len, ATTN_KEY_CHUNK)]
    chunks.append((kc_ref, vct_scr, 0, c_len, s_len))
    units = [(r0, pr * pair_w) for pr in range(GQA_GROUP // 2)
             for r0 in range(0, q_ref.shape[1], ATTN_SUB)]

    def stack_pair(ref, row0, col0):
        rows = slice(row0, row0 + ATTN_SUB)
        return jnp.concatenate([ref[0, rows, col0:col0 + HEAD_DIM],
                                ref[0, rows, col0 + HEAD_DIM:col0 + pair_w]], axis=0)

    def fold8(x, op):
        return op(x.reshape(x.shape[0] // SUBLANES, SUBLANES, x.shape[1]), axis=0)

    def chunk_scores(q2, slot, chunk, mpart):
        kref, _, r, n, row = chunk
        s = lax.dot_general(kref[0, r:r + n, :], q2, nt, preferred_element_type=F32)
        s_scr[slot, row:row + n, :] = s
        cm = fold8(s, jnp.max)
        return cm if mpart is None else jnp.maximum(mpart, cm)

    def phase(q_next, slot_next, slot_cur, unit):
        row0, col0 = unit
        m_cur = m_scr[slot_cur]
        mpart = lpart = ot = None
        for chunk in chunks:
            _, vtref, r, n, row = chunk
            mpart = chunk_scores(q_next, slot_next, chunk, mpart)
            p = jnp.exp2(s_scr[slot_cur, row:row + n, :] - m_cur)
            cl = fold8(p, jnp.sum)
            lpart = cl if lpart is None else lpart + cl
            pv = _dot(vtref[:, r:r + n], p.astype(BF16))
            ot = pv if ot is None else ot + pv
        m_scr[slot_next] = jnp.max(mpart, axis=0, keepdims=True)
        ot = ot * (1.0 / jnp.sum(lpart, axis=0, keepdims=True))
        for e in range(2):
            o_ref[0, row0:row0 + ATTN_SUB, col0 + e * HEAD_DIM:col0 + (e + 1) * HEAD_DIM] = (
                ot[:, e * ATTN_SUB:(e + 1) * ATTN_SUB].T.astype(BF16))

    @pl.when(i == 0)
    def _():
        vt_scr[...] = v_ref[0].T
        vct_scr[...] = vc_ref[0].T
        q2 = stack_pair(q_ref, *units[0])
        mpart = None
        for chunk in chunks:
            mpart = chunk_scores(q2, 0, chunk, mpart)
        m_scr[0] = jnp.max(mpart, axis=0, keepdims=True)

    for u, unit in enumerate(units):
        if u + 1 < len(units):
            q_next = stack_pair(q_ref, *units[u + 1])
        else:
            q_next = stack_pair(qn_ref, 0, 0)
        phase(q_next, (u + 1) % 2, u % 2, unit)


def _attention(p, pc):
    b, s, _ = p.shape
    c = pc.shape[1]
    gw = GQA_GROUP * HEAD_DIM
    kb = OFF_K // HEAD_DIM
    vb = OFF_V // HEAD_DIM
    n = s // ATTN_TQ
    return pl.pallas_call(
        _attn_kernel,
        grid=(b, N_KV_HEADS, n),
        in_specs=[
            pl.BlockSpec((1, ATTN_TQ, gw), lambda bb, g, i: (bb, i, g)),
            pl.BlockSpec((1, ATTN_SUB, 2 * HEAD_DIM),
                         lambda bb, g, i: (bb, jnp.minimum(i + 1, n - 1) * (ATTN_TQ // ATTN_SUB),
                                           2 * g)),
            pl.BlockSpec((1, s, HEAD_DIM), lambda bb, g, i: (bb, 0, kb + g)),
            pl.BlockSpec((1, s, HEAD_DIM), lambda bb, g, i: (bb, 0, vb + g)),
            pl.BlockSpec((1, c, HEAD_DIM), lambda bb, g, i: (bb, 0, g)),
            pl.BlockSpec((1, c, HEAD_DIM), lambda bb, g, i: (bb, 0, N_KV_HEADS + g)),
        ],
        out_specs=pl.BlockSpec((1, ATTN_TQ, gw), lambda bb, g, i: (bb, i, g)),
        out_shape=jax.ShapeDtypeStruct((b, s, ATTN_W), BF16),
        scratch_shapes=[
            pltpu.VMEM((HEAD_DIM, s), BF16),
            pltpu.VMEM((HEAD_DIM, c), BF16),
            pltpu.VMEM((2, s + c, 2 * ATTN_SUB), F32),
            pltpu.VMEM((2, 1, 2 * ATTN_SUB), F32),
        ],
        compiler_params=_params("arbitrary", "arbitrary", "arbitrary"),
        name="attention",
    )(p, p, p, p, pc, pc)


LRU_CB = 512
LRU_TL = 256
LRU_TINY = 1e-30
LRU_GROUP = 2
HALO = SUBLANES
CONV_LEFT = 2


def _gelu_tanh(x):
    return 0.5 * x * (1.0 + jnp.tanh(0.7978845608028654 * (x + 0.044715 * x * x * x)))


def _token_conv(x_ref, xp_ref, xn_ref, cw_ref, cb_ref, ci, n, nb):
    tl, cb = x_ref.shape[1], x_ref.shape[2]
    ri = lax.broadcasted_iota(jnp.int32, (tl, tl), 0)
    cj = lax.broadcasted_iota(jnp.int32, (tl, tl), 1)
    shifts = jnp.concatenate(
        [(cj == ri + d).astype(BF16) for d in (-2, -1, 1)], axis=0)
    row8 = lax.broadcasted_iota(jnp.int32, (HALO, cb), 0)
    has_prev = (ci > 0).astype(F32)
    has_next = (ci < n - 1).astype(F32)
    w0, w1, w2, w3 = (cw_ref[k:k + 1, :] for k in range(4))
    xcs = []
    for bb in range(nb):
        xb = x_ref[bb]
        sh = _dot(shifts, xb)
        y = (cb_ref[...] + w2 * xb.astype(F32) + w0 * sh[:tl] + w1 * sh[tl:2 * tl]
             + w3 * sh[2 * tl:])
        prev = xp_ref[bb].astype(F32) * has_prev
        nxt = xn_ref[bb].astype(F32) * has_next
        head = (w0 * jnp.where(row8 < 2, pltpu.roll(prev, 2, axis=0), 0.0)
                + w1 * jnp.where(row8 < 1, pltpu.roll(prev, 1, axis=0), 0.0))
        tail = w3 * jnp.where(row8 == HALO - 1, pltpu.roll(nxt, HALO - 1, axis=0), 0.0)
        xcs.append(jnp.concatenate(
            [y[:HALO] + head, y[HALO:tl - HALO], y[tl - HALO:] + tail], axis=0))
    return jnp.concatenate(xcs, axis=0)


def _lru_kernel(*refs, reverse, add_other, nb, conv_from):
    refs = list(refs)
    if conv_from == "given":
        xc_ref = refs.pop(0)
        tl, cb = xc_ref.shape[1], xc_ref.shape[2]
    else:
        x_ref, xp_ref, xn_ref, cw_ref, cb_ref = refs[:5]
        del refs[:5]
        tl, cb = x_ref.shape[1], x_ref.shape[2]
    w_ref, ba_ref, bx_ref, lam_ref, h0_ref = refs[:5]
    del refs[:5]
    if add_other:
        hf_ref = refs.pop(0)
    o_ref, hl_ref = refs[:2]
    del refs[:2]
    if conv_from == "input+emit":
        xc_out_ref = refs.pop(0)
    a_scr, u_scr, ho_scr, h_scr = refs
    i = pl.program_id(1)
    n = pl.num_programs(1)
    ci = n - 1 - i if reverse else i

    def slot_rows(blk, bb):
        return pl.ds((blk % LRU_GROUP) * nb + bb, tl, stride=SUBLANES)

    @pl.when(i == 0)
    def _():
        for blk in range(cb // LRU_BLOCK_DIM):
            e = blk % LRU_GROUP
            h_scr[blk // LRU_GROUP, e * nb:(e + 1) * nb, :] = (
                h0_ref[:, blk * LRU_BLOCK_DIM:(blk + 1) * LRU_BLOCK_DIM])

    if conv_from == "given":
        xc = xc_ref[...].reshape(nb * tl, cb)
    else:
        xc = _token_conv(x_ref, xp_ref, xn_ref, cw_ref, cb_ref, ci, n, nb)
        if conv_from == "input+emit":
            xc_out_ref[...] = xc.reshape(nb, tl, cb)

    lam = lam_ref[...]
    neg = -lam
    sp = jnp.maximum(neg, 0.0) + jnp.log1p(jnp.exp(-jnp.abs(neg)))
    half_c = (-0.5 * LRU_C * LOG2_E) * sp
    for blk in range(cb // LRU_BLOCK_DIM):
        sl = slice(blk * LRU_BLOCK_DIM, (blk + 1) * LRU_BLOCK_DIM)
        xb = xc[:, sl]
        z = _dot(xb.astype(BF16), w_ref[blk])
        tr = jnp.tanh(z[:, :LRU_BLOCK_DIM] + ba_ref[:, sl])
        ti = jnp.tanh(z[:, LRU_BLOCK_DIM:] + bx_ref[:, sl])
        hc = half_c[:, sl]
        a = jnp.exp2(hc + hc * tr)
        hx = 0.5 * xb
        y = 1.0 - a * a
        root = y * lax.rsqrt(jnp.maximum(y, LRU_TINY))
        u = root * (hx + hx * ti)
        for bb in range(nb):
            a_scr[blk // LRU_GROUP, slot_rows(blk, bb), :] = a[bb * tl:(bb + 1) * tl]
            u_scr[blk // LRU_GROUP, slot_rows(blk, bb), :] = u[bb * tl:(bb + 1) * tl]

    ngrp = cb // LRU_BLOCK_DIM // LRU_GROUP

    def step(t, hs):
        pair = tl // 2 - 1 - t if reverse else t
        base = pl.multiple_of(pair * (2 * SUBLANES), 2 * SUBLANES)
        first, second = (SUBLANES, 0) if reverse else (0, SUBLANES)
        rows1 = pl.ds(base + first, SUBLANES)
        rows2 = pl.ds(base + second, SUBLANES)
        out = []
        for g in range(ngrp):
            a1 = a_scr[g, rows1, :]
            u1 = u_scr[g, rows1, :]
            a2 = a_scr[g, rows2, :]
            u2 = u_scr[g, rows2, :]
            h2 = (a2 * a1) * hs[g] + (a2 * u1 + u2)
            ho_scr[g, rows1, :] = a1 * hs[g] + u1
            ho_scr[g, rows2, :] = h2
            out.append(h2)
        return tuple(out)

    hs = lax.fori_loop(0, tl // 2, step, tuple(h_scr[g] for g in range(ngrp)), unroll=4)
    for g in range(ngrp):
        h_scr[g] = hs[g]
    for blk in range(cb // LRU_BLOCK_DIM):
        e = blk % LRU_GROUP
        hl_ref[:, blk * LRU_BLOCK_DIM:(blk + 1) * LRU_BLOCK_DIM] = (
            hs[blk // LRU_GROUP][e * nb:(e + 1) * nb])

    for bb in range(nb):
        hb = jnp.concatenate(
            [ho_scr[blk // LRU_GROUP, slot_rows(blk, bb), :]
             for blk in range(cb // LRU_BLOCK_DIM)], axis=-1)
        if add_other:
            hb = hb + hf_ref[bb].astype(F32)
        o_ref[bb] = hb.astype(BF16)


def _lru(p, x_col0, conv_w, conv_b, w_blk, ba, bx, lam, h0, *, reverse, emit_conv=False,
         conv=None, hf=None):
    nb, t, _ = p.shape
    tl = min(LRU_TL, t)
    n = t // tl
    cb = LRU_CB
    xb0 = x_col0 // cb
    hpc = tl // HALO
    assert nb * LRU_GROUP == SUBLANES
    ngrp = cb // LRU_BLOCK_DIM // LRU_GROUP
    conv_from = "given" if conv is not None else ("input+emit" if emit_conv else "input")

    def pos(i):
        return n - 1 - i if reverse else i

    chunk_spec = lambda col0: pl.BlockSpec((nb, tl, cb), lambda c, i: (0, pos(i), col0 + c))
    row_spec = lambda rows: pl.BlockSpec((rows, cb), lambda c, i: (0, c))
    if conv is not None:
        in_specs = [chunk_spec(0)]
        args = [conv]
    else:
        in_specs = [
            chunk_spec(xb0),
            pl.BlockSpec((nb, HALO, cb),
                         lambda c, i: (0, jnp.maximum(pos(i) * hpc - 1, 0), xb0 + c)),
            pl.BlockSpec((nb, HALO, cb),
                         lambda c, i: (0, jnp.minimum((pos(i) + 1) * hpc, t // HALO - 1), xb0 + c)),
            row_spec(4),
            row_spec(1),
        ]
        args = [p, p, p, conv_w, conv_b.reshape(1, -1)]
    in_specs += [
        pl.BlockSpec((cb // LRU_BLOCK_DIM, LRU_BLOCK_DIM, 2 * LRU_BLOCK_DIM), lambda c, i: (c, 0, 0)),
        row_spec(1), row_spec(1), row_spec(1), row_spec(nb),
    ]
    args += [w_blk, ba.reshape(1, -1), bx.reshape(1, -1), lam.reshape(1, -1), h0]
    if hf is not None:
        in_specs.append(chunk_spec(0))
        args.append(hf)
    out_specs = [chunk_spec(0), row_spec(nb)]
    out_shape = [jax.ShapeDtypeStruct((nb, t, LRU_W), BF16), jax.ShapeDtypeStruct((nb, LRU_W), F32)]
    if emit_conv:
        out_specs.append(chunk_spec(0))
        out_shape.append(jax.ShapeDtypeStruct((nb, t, LRU_W), F32))
    kern = functools.partial(_lru_kernel, reverse=reverse, add_other=hf is not None, nb=nb,
                             conv_from=conv_from)
    return pl.pallas_call(
        kern,
        grid=(LRU_W // cb, n),
        in_specs=in_specs,
        out_specs=out_specs,
        out_shape=out_shape,
        scratch_shapes=[
            pltpu.VMEM((ngrp, tl * SUBLANES, LRU_BLOCK_DIM), F32),
            pltpu.VMEM((ngrp, tl * SUBLANES, LRU_BLOCK_DIM), F32),
            pltpu.VMEM((ngrp, tl * SUBLANES, LRU_BLOCK_DIM), F32),
            pltpu.VMEM((ngrp, SUBLANES, LRU_BLOCK_DIM), F32),
        ],
        compiler_params=_params("parallel", "arbitrary"),
        name="lru_bwd" if reverse else "lru_fwd",
    )(*args)


MERGE_TM = 1024
MERGE_TK = 1024
MERGE_ROWS = 128


def _merge_kernel(mod_ref, attn_ref, hsum_ref, lg_ref, ga_ref, gl_ref, w_ref, o_ref, acc_scr):
    kk = pl.program_id(2)

    def partial_products(k_half):
        w = w_ref[k_half * MERGE_TK:(k_half + 1) * MERGE_TK, :]
        for r0 in range(0, acc_scr.shape[0], MERGE_ROWS):
            rows = slice(r0, r0 + MERGE_ROWS)
            lru = (hsum_ref[0, rows, :].astype(F32)
                   * _gelu_tanh(lg_ref[0, rows, :].astype(F32))).astype(BF16)
            m = (attn_ref[0, rows, :] * (1.0 + jnp.tanh(ga_ref[0, rows, :]))
                 + lru * (1.0 + jnp.tanh(gl_ref[0, rows, :])))
            yield rows, _dot(m, w)

    @pl.when(kk == 0)
    def _():
        for rows, part in partial_products(0):
            acc_scr[rows, :] = part

    @pl.when(kk == 1)
    def _():
        gate = mod_ref[0, 5:6, :]
        for rows, part in partial_products(1):
            o_ref[0, rows, :] = (gate * (acc_scr[rows, :] + part)).astype(BF16)


def _merge(mod, attn, hsum, p, w_out):
    b, s, d = attn.shape
    lg0 = OFF_LG // MERGE_TK
    ga0 = OFF_GA // MERGE_TK
    gl0 = OFF_GL // MERGE_TK
    return pl.pallas_call(
        _merge_kernel,
        grid=(b, s // MERGE_TM, d // MERGE_TK),
        in_specs=[
            pl.BlockSpec((1, N_MOD, d), lambda bb, i, k: (bb, 0, 0)),
            pl.BlockSpec((1, MERGE_TM, MERGE_TK), lambda bb, i, k: (bb, i, k)),
            pl.BlockSpec((1, MERGE_TM, MERGE_TK), lambda bb, i, k: (bb, i, k)),
            pl.BlockSpec((1, MERGE_TM, MERGE_TK), lambda bb, i, k: (bb, i, lg0 + k)),
            pl.BlockSpec((1, MERGE_TM, MERGE_TK), lambda bb, i, k: (bb, i, ga0 + k)),
            pl.BlockSpec((1, MERGE_TM, MERGE_TK), lambda bb, i, k: (bb, i, gl0 + k)),
            pl.BlockSpec((d, d), lambda bb, i, k: (0, 0)),
        ],
        out_specs=pl.BlockSpec((1, MERGE_TM, d), lambda bb, i, k: (bb, i, 0)),
        out_shape=jax.ShapeDtypeStruct((b, s, d), BF16),
        scratch_shapes=[pltpu.VMEM((MERGE_TM, d), F32)],
        compiler_params=_params("parallel", "parallel", "arbitrary"),
        name="merge",
    )(mod, attn, hsum, p, p, p, w_out)


def _deinterleave_heads(w, n_heads):
    lead = w.shape[:-1]
    w = w.reshape(lead + (n_heads, HEAD_DIM // 2, 2))
    w = jnp.swapaxes(w, -1, -2)
    return w.reshape(lead + (n_heads * HEAD_DIM,))


def _rope_tables(n_tok):
    rows = n_tok // GRID_W
    row = jnp.repeat(jnp.arange(rows, dtype=F32), GRID_W)
    col = jnp.tile(jnp.arange(GRID_W, dtype=F32), rows)
    axis_dims = HEAD_DIM // 2
    freqs = ROPE_THETA ** (-jnp.arange(0, axis_dims, 2, dtype=F32) / axis_dims)
    ang = jnp.concatenate([row[:, None] * freqs, col[:, None] * freqs], axis=-1)
    cos, sin = jnp.cos(ang), jnp.sin(ang)
    return jnp.concatenate([cos, cos], axis=-1), jnp.concatenate([-sin, sin], axis=-1)


def kernel(x, c, ctx, c_ctx, w_mod, b_mod, norm_g, ffn_wg, ffn_wu, ffn_wd, w_in, w_out, q_norm_g,
           k_norm_g, conv_w, conv_b, lru_wa, lru_ba, lru_wx, lru_bx, lru_lambda, final_norm_g):
    b, s, d = x.shape
    n_ctx = ctx.shape[1]
    ctx_row = b

    cc = jnp.concatenate([c, c_ctx[None], jnp.zeros((SUBLANES - b - 1, d), F32)], axis=0)
    mod = _modulation(cc, w_mod[0], b_mod[0]).reshape(SUBLANES, N_MOD, d)

    wg = ffn_wg[0].astype(BF16)
    wu = ffn_wu[0].astype(BF16)
    wd = ffn_wd[0].astype(BF16)
    w_in_p = _prep_w_in(w_in[0])
    w_out_b = (0.5 * w_out[0]).astype(BF16)
    qkg = jnp.stack([
        _deinterleave_heads(q_norm_g[0], 1) * (HEAD_DIM ** -0.5 * LOG2_E),
        _deinterleave_heads(k_norm_g[0], 1),
    ])
    cosf, sinf = _rope_tables(s)
    w_lru = (0.5 * jnp.concatenate([lru_wa[0], lru_wx[0]], axis=-1)).astype(BF16)
    ba_h = 0.5 * lru_ba[0]
    bx_h = 0.5 * lru_bx[0]

    lat_row = lambda bb: bb
    ctx_rowf = lambda bb: ctx_row

    x1 = _ffn(x, mod, lat_row, norm_g[0, 0], wg, wu, wd, final_norm_g,
              ffn_idx=0, mod_base=0, final_norm=False, tm=512)
    ctx_flat = ctx.reshape(1, b * n_ctx, d)
    ctx1 = _ffn(ctx_flat, mod, ctx_rowf, norm_g[0, 0], wg, wu, wd, final_norm_g,
                ffn_idx=0, mod_base=0, final_norm=False, tm=512)

    pc = _proj(ctx1, mod, ctx_rowf, norm_g[0, 1], w_in_p, qkg, cosf, sinf,
               col0=PROJ_TILE_KV, ncol=3, rope=False, tm=b * n_ctx)
    pc = pc.reshape(b, n_ctx, 3 * PROJ_TN)
    zeros = jnp.zeros((b, LRU_W), F32)
    lx_c = OFF_LX - OFF_K
    lru_f = (w_lru[0], ba_h[0], bx_h[0], lru_lambda[0, 0])
    lru_b = (w_lru[1], ba_h[1], bx_h[1], lru_lambda[0, 1])
    _, hf0 = _lru(pc, lx_c, conv_w[0], conv_b[0], *lru_f, zeros, reverse=False)
    _, hb0 = _lru(pc, lx_c, conv_w[0], conv_b[0], *lru_b, zeros, reverse=True)

    p = _proj(x1, mod, lat_row, norm_g[0, 1], w_in_p, qkg, cosf, sinf,
              col0=0, ncol=IN_W // PROJ_TN, rope=True, tm=1024)
    attn = _attention(p, pc)
    hf, _, xc = _lru(p, OFF_LX, conv_w[0], conv_b[0], *lru_f, hf0, reverse=False, emit_conv=True)
    hsum, _ = _lru(p, OFF_LX, conv_w[0], conv_b[0], *lru_b, hb0, reverse=True, conv=xc, hf=hf)
    mix = _merge(mod, attn, hsum, p, w_out_b)

    return _ffn(x1, mod, lat_row, norm_g[0, 2], wg, wu, wd, final_norm_g,
                ffn_idx=1, mod_base=6, final_norm=True, tm=512, delta=mix)
```
